```python
import math
import jax, jax.numpy as jnp
from jax import lax
import numpy as np

D_MODEL = 1024
BATCH = 8
SEQ = 2048
DEPTH = 1
DEC_BATCH = 32
DEC_SEQ = 4
PAST_LEN = 8192
PAGE_SIZE = 128

HEAD_DIM = 64
GDN_HEADS = 8
SB_HEADS = 8
GDN_WIDTH = GDN_HEADS * HEAD_DIM
SB_WIDTH = SB_HEADS * HEAD_DIM
MIX_WIDTH = GDN_WIDTH + SB_WIDTH
CONV_WIDTH = 4
CONV_DIM = 3 * GDN_WIDTH
GDN_CHUNK = 64
SB_BLOCK = 128
SB_BIAS_INIT = -8.0
IN_DIM = CONV_DIM + GDN_WIDTH + 2 * GDN_HEADS + 3 * SB_WIDTH
N_EXPERTS = 32
TOP_K = 4
D_FF = D_MODEL
MOE_BLOCK = 128
SWIGLU_LIMIT = 7.0
SWIGLU_ALPHA = 1.702
RMS_EPS = 1e-6
L2_EPS = 1e-6

kernel_name = 'hymba_gdn_stickbreaking_moe_step'


def rms_norm(x, w):
    xf = x.astype(jnp.float32)
    y = xf * lax.rsqrt(jnp.mean(xf * xf, axis=-1, keepdims=True) + RMS_EPS)
    return (y * w.astype(jnp.float32)).astype(x.dtype)


def l2_normalize(x):
    return x * lax.rsqrt(jnp.sum(x * x, axis=-1, keepdims=True) + L2_EPS)


def causal_short_conv(u, buf, conv_w):
    length = u.shape[1]
    full = jnp.concatenate([buf.astype(u.dtype), u], axis=1)
    out = sum(full[:, i:i + length] * conv_w[i] for i in range(CONV_WIDTH))
    return jax.nn.silu(out), full[:, -(CONV_WIDTH - 1):]


def gated_delta_chunked(q, k, v, g, beta, s0, chunk):
    b, l, h, _ = q.shape
    n = l // chunk

    def blocks(t):
        t = t.reshape((b, n, chunk, h) + t.shape[3:])
        return jnp.moveaxis(t, 1, 0).swapaxes(2, 3)

    qc, kc, vc, gc, bc = (blocks(t) for t in (q, k, v, g, beta))
    G = jnp.cumsum(gc, axis=-1)
    incl = jnp.tril(jnp.ones((chunk, chunk), bool))
    strict = jnp.tril(jnp.ones((chunk, chunk), bool), -1)
    diff = G[..., :, None] - G[..., None, :]
    decay = jnp.where(incl, jnp.exp(jnp.where(incl, diff, 0.0)), 0.0)
    kb = kc * bc[..., None]
    low = jnp.where(strict, jnp.einsum('nbhid,nbhjd->nbhij', kb, kc) * decay, 0.0)
    eye = jnp.eye(chunk, dtype=low.dtype)
    t_inv = lax.linalg.triangular_solve(eye + low, jnp.broadcast_to(eye, low.shape),
                                        left_side=True, lower=True)
    u = jnp.einsum('nbhij,nbhjd->nbhid', t_inv, vc * bc[..., None])
    w = jnp.einsum('nbhij,nbhjd->nbhid', t_inv, kb * jnp.exp(G)[..., None])
    qk = jnp.where(incl, jnp.einsum('nbhid,nbhjd->nbhij', qc, kc) * decay, 0.0)
    g_last = G[..., -1]
    k_tail = kc * jnp.exp(g_last[..., None] - G)[..., None]
    q_head = qc * jnp.exp(G)[..., None]

    def step(state, xs):
        u_c, w_c, qk_c, qh_c, kt_c, gl_c = xs
        v_new = u_c - jnp.einsum('bhcd,bhde->bhce', w_c, state)
        o = jnp.einsum('bhcd,bhde->bhce', qh_c, state) + jnp.einsum('bhij,bhje->bhie', qk_c, v_new)
        state = state * jnp.exp(gl_c)[..., None, None] + jnp.einsum('bhcd,bhce->bhde', kt_c, v_new)
        return state, o

    s_final, o = lax.scan(step, s0.astype(jnp.float32), (u, w, qk, q_head, k_tail, g_last))
    o = jnp.moveaxis(o.swapaxes(2, 3), 0, 1).reshape(b, l, h, -1)
    return o, s_final


def stick_breaking(q, k, v, past_len, bias):
    b, l, h, d = q.shape
    key_pos = jnp.arange(k.shape[1])
    blk = SB_BLOCK if l % SB_BLOCK == 0 else l
    nb = l // blk
    q_blocks = jnp.moveaxis(q.reshape(b, nb, blk, h, d), 1, 0)
    q_pos = past_len + jnp.arange(l).reshape(nb, blk)
    scale = 1.0 / math.sqrt(d)
    bias_f = bias.astype(jnp.float32)[None, :, None, None]

    def one_block(args):
        q_blk, pos = args
        z = jnp.einsum('bqhd,bkhd->bhqk', q_blk, k, preferred_element_type=jnp.float32) * scale + bias_f
        mask = key_pos[None, :] < pos[:, None]
        log_rem = jnp.where(mask, jax.nn.log_sigmoid(-z), 0.0)
        after = lax.cumsum(log_rem, axis=3, reverse=True) - log_rem
        a = jnp.where(mask, jnp.exp(jax.nn.log_sigmoid(z) + after), 0.0)
        return jnp.einsum('bhqk,bkhd->bqhd', a.astype(v.dtype), v)

    o = lax.map(one_block, (q_blocks, q_pos))
    return jnp.moveaxis(o, 0, 1).reshape(b, l, h, d)


def mixer_layer(h, conv_buf, s0, past_k, past_v, w_in, conv_w, a_log, dt_bias, gdn_norm, sb_bias, w_out):
    b, l, _ = h.shape
    proj = h @ w_in
    c1 = CONV_DIM
    c2 = c1 + GDN_WIDTH
    c3 = c2 + GDN_HEADS
    c4 = c3 + GDN_HEADS
    qkv_g, z_g, beta_in, a_in, qkv_s = jnp.split(proj, [c1, c2, c3, c4], axis=-1)
    conv_out, conv_new = causal_short_conv(qkv_g, conv_buf, conv_w)
    qg, kg, vg = (t.reshape(b, l, GDN_HEADS, HEAD_DIM).astype(jnp.float32)
                  for t in jnp.split(conv_out, 3, axis=-1))
    qg = l2_normalize(qg) * (HEAD_DIM ** -0.5)
    kg = l2_normalize(kg)
    beta = jax.nn.sigmoid(beta_in.astype(jnp.float32))
    g = -jnp.exp(a_log.astype(jnp.float32)) * jax.nn.softplus(a_in.astype(jnp.float32) + dt_bias.astype(jnp.float32))
    chunk = GDN_CHUNK if l % GDN_CHUNK == 0 else l
    o_g, s_new = gated_delta_chunked(qg, kg, vg, g, beta, s0, chunk)
    gate = jax.nn.silu(z_g.astype(jnp.float32)).reshape(b, l, GDN_HEADS, HEAD_DIM)
    o_g = (rms_norm(o_g, gdn_norm) * gate).reshape(b, l, GDN_WIDTH)
    qs, ks, vs = (t.reshape(b, l, SB_HEADS, HEAD_DIM) for t in jnp.split(qkv_s, 3, axis=-1))
    k_all = jnp.concatenate([past_k.astype(ks.dtype), ks], axis=1)
    v_all = jnp.concatenate([past_v.astype(vs.dtype), vs], axis=1)
    o_s = stick_breaking(qs, k_all, v_all, past_k.shape[1], sb_bias).reshape(b, l, SB_WIDTH)
    out = jnp.concatenate([o_g.astype(h.dtype), o_s.astype(h.dtype)], axis=-1) @ w_out
    return out, ks, vs, conv_new, s_new


def moe_ffn(h, router_w, router_b, w_up, b_up, w_down, b_down):
    shp = h.shape
    t = h.reshape(-1, shp[-1])
    n_tok = t.shape[0]
    logits = jnp.dot(t, router_w, preferred_element_type=jnp.float32) + router_b.astype(jnp.float32)
    top_v, top_i = lax.top_k(logits, TOP_K)
    gates = jax.nn.softmax(top_v, axis=-1)
    n_pairs = n_tok * TOP_K
    pair_e = top_i.reshape(-1)
    pair_tok = jnp.arange(n_pairs, dtype=jnp.int32) // TOP_K
    pair_w = gates.reshape(-1)
    order = jnp.argsort(pair_e)
    e_sorted = pair_e[order]
    counts = jnp.bincount(pair_e, length=N_EXPERTS)
    padded = (counts + MOE_BLOCK - 1) // MOE_BLOCK * MOE_BLOCK
    start = jnp.cumsum(counts) - counts
    ends_pad = jnp.cumsum(padded)
    start_pad = ends_pad - padded
    dest = start_pad[e_sorted] + jnp.arange(n_pairs) - start[e_sorted]
    n_slots = (-(-n_pairs // MOE_BLOCK) + N_EXPERTS) * MOE_BLOCK
    n_blocks = n_slots // MOE_BLOCK
    slot_tok = jnp.zeros((n_slots,), jnp.int32).at[dest].set(pair_tok[order])
    slot_w = jnp.zeros((n_slots,), jnp.float32).at[dest].set(pair_w[order])
    block_e = jnp.minimum(jnp.searchsorted(ends_pad, jnp.arange(n_blocks) * MOE_BLOCK, side='right'),
                          N_EXPERTS - 1)

    def expert_block(args):
        tok, e = args
        hu = t[tok] @ w_up[e] + b_up[e]
        gate = jnp.minimum(hu[:, ::2], SWIGLU_LIMIT)
        up = jnp.clip(hu[:, 1::2], -SWIGLU_LIMIT, SWIGLU_LIMIT)
        act = (up + 1.0) * gate * jax.nn.sigmoid(SWIGLU_ALPHA * gate)
        return act @ w_down[e] + b_down[e]

    out = lax.map(expert_block, (slot_tok.reshape(n_blocks, MOE_BLOCK), block_e))
    out = out.reshape(n_slots, -1).astype(jnp.float32) * slot_w[:, None]
    y = jax.ops.segment_sum(out, slot_tok, num_segments=n_tok)
    return y.reshape(shp).astype(h.dtype)


def setup_inputs(seed: int = 0) -> dict:
    key = jax.random.key(seed)
    ks = jax.random.split(key, 24)
    f32 = jnp.float32
    n_pages = PAST_LEN // PAGE_SIZE
    used = DEC_BATCH * n_pages
    n_pool = used + used // 4

    def nrm(k, shape, scale):
        return jax.random.normal(k, shape, f32) * scale

    x_prompt = nrm(ks[0], (BATCH, SEQ, D_MODEL), 1.0)
    x_sample = nrm(ks[1], (DEC_BATCH, DEC_SEQ, D_MODEL), 1.0)
    cache_k = nrm(ks[2], (DEPTH, n_pool, PAGE_SIZE, SB_HEADS, HEAD_DIM), 1.0)
    cache_v = nrm(ks[3], (DEPTH, n_pool, PAGE_SIZE, SB_HEADS, HEAD_DIM), 1.0)
    page_table = jax.random.permutation(ks[4], n_pool)[:used].reshape(DEC_BATCH, n_pages).astype(jnp.int32)
    state_conv = nrm(ks[5], (DEPTH, DEC_BATCH, CONV_WIDTH - 1, CONV_DIM), 1.0)
    state_ssm = nrm(ks[6], (DEPTH, DEC_BATCH, GDN_HEADS, HEAD_DIM, HEAD_DIM), 0.1)
    norm_mix = 1.0 + nrm(ks[7], (DEPTH, D_MODEL), 0.02)
    w_in = nrm(ks[8], (DEPTH, D_MODEL, IN_DIM), D_MODEL ** -0.5)
    conv_w = nrm(ks[9], (DEPTH, CONV_WIDTH, CONV_DIM), CONV_WIDTH ** -0.5)
    a_log = jnp.log(jax.random.uniform(ks[10], (DEPTH, GDN_HEADS), f32, 1.0, 16.0))
    dt_bias = jnp.log(jnp.expm1(jax.random.uniform(ks[11], (DEPTH, GDN_HEADS), f32, 0.001, 0.1)))
    gdn_norm = 1.0 + nrm(ks[12], (DEPTH, HEAD_DIM), 0.02)
    sb_bias = SB_BIAS_INIT + nrm(ks[22], (DEPTH, SB_HEADS), 0.1)
    w_out = nrm(ks[13], (DEPTH, MIX_WIDTH, D_MODEL), MIX_WIDTH ** -0.5)
    norm_ffn = 1.0 + nrm(ks[14], (DEPTH, D_MODEL), 0.02)
    router_w = nrm(ks[15], (DEPTH, D_MODEL, N_EXPERTS), D_MODEL ** -0.5)
    router_b = nrm(ks[16], (DEPTH, N_EXPERTS), 0.01)
    w_up = nrm(ks[17], (DEPTH, N_EXPERTS, D_MODEL, 2 * D_FF), D_MODEL ** -0.5)
    b_up = nrm(ks[18], (DEPTH, N_EXPERTS, 2 * D_FF), 0.01)
    w_down = nrm(ks[19], (DEPTH, N_EXPERTS, D_FF, D_MODEL), D_FF ** -0.5)
    b_down = nrm(ks[20], (DEPTH, N_EXPERTS, D_MODEL), 0.01)
    norm_final = 1.0 + nrm(ks[21], (D_MODEL,), 0.02)
    return {'x_prompt': x_prompt, 'x_sample': x_sample, 'cache_k': cache_k, 'cache_v': cache_v,
            'page_table': page_table, 'state_conv': state_conv, 'state_ssm': state_ssm,
            'norm_mix': norm_mix, 'w_in': w_in, 'conv_w': conv_w, 'a_log': a_log, 'dt_bias': dt_bias,
            'gdn_norm': gdn_norm, 'sb_bias': sb_bias, 'w_out': w_out, 'norm_ffn': norm_ffn,
            'router_w': router_w, 'router_b': router_b, 'w_up': w_up, 'b_up': b_up, 'w_down': w_down,
            'b_down': b_down, 'norm_final': norm_final}


def reference(x_prompt, x_sample, cache_k, cache_v, page_table, state_conv, state_ssm,
              norm_mix, w_in, conv_w, a_log, dt_bias, gdn_norm, sb_bias, w_out,
              norm_ffn, router_w, router_b, w_up, b_up, w_down, b_down, norm_final):
    xp, xs = x_prompt, x_sample
    bp, bs = xp.shape[0], xs.shape[0]
    k_p, v_p, c_p, s_p, k_s, v_s, c_s, s_s = ([] for _ in range(8))
    for l in range(DEPTH):
        mix = (w_in[l], conv_w[l], a_log[l], dt_bias[l], gdn_norm[l], sb_bias[l], w_out[l])
        ffn = (router_w[l], router_b[l], w_up[l], b_up[l], w_down[l], b_down[l])
        zero_conv = jnp.zeros((bp, CONV_WIDTH - 1, CONV_DIM), xp.dtype)
        zero_ssm = jnp.zeros((bp, GDN_HEADS, HEAD_DIM, HEAD_DIM), jnp.float32)
        no_past = jnp.zeros((bp, 0, SB_HEADS, HEAD_DIM), xp.dtype)
        out_p, kp, vp, cp, sp = mixer_layer(rms_norm(xp, norm_mix[l]), zero_conv, zero_ssm,
                                            no_past, no_past, *mix)
        xp = xp + out_p
        xp = xp + moe_ffn(rms_norm(xp, norm_ffn[l]), *ffn)
        past_k = cache_k[l][page_table].reshape(bs, -1, SB_HEADS, HEAD_DIM)
        past_v = cache_v[l][page_table].reshape(bs, -1, SB_HEADS, HEAD_DIM)
        out_s, ksn, vsn, csn, ssn = mixer_layer(rms_norm(xs, norm_mix[l]), state_conv[l], state_ssm[l],
                                                past_k, past_v, *mix)
        xs = xs + out_s
        xs = xs + moe_ffn(rms_norm(xs, norm_ffn[l]), *ffn)
        k_p.append(kp); v_p.append(vp); c_p.append(cp); s_p.append(sp)
        k_s.append(ksn); v_s.append(vsn); c_s.append(csn); s_s.append(ssn)
    y_prompt = rms_norm(xp, norm_final)
    y_sample = rms_norm(xs, norm_final)
    return (y_prompt, y_sample,
            jnp.stack(k_p), jnp.stack(v_p), jnp.stack(c_p), jnp.stack(s_p),
            jnp.stack(k_s), jnp.stack(v_s), jnp.stack(c_s), jnp.stack(s_s))
```

```python
import functools
import math

import jax
import jax.numpy as jnp
from jax import lax
from jax.experimental import pallas as pl
from jax.experimental.pallas import tpu as pltpu

F32 = jnp.float32
BF16 = jnp.bfloat16

HEAD_DIM = 64
GDN_HEADS = 8
SB_HEADS = 8
GDN_WIDTH = GDN_HEADS * HEAD_DIM
SB_WIDTH = SB_HEADS * HEAD_DIM
CONV_WIDTH = 4
CONV_DIM = 3 * GDN_WIDTH
GDN_CHUNK = 64
TOP_K = 4
SWIGLU_LIMIT = 7.0
SWIGLU_ALPHA = 1.702
RMS_EPS = 1e-6
L2_EPS = 1e-6
LANES = 128
SUBLANES = 8
VMEM_LIMIT = 56 * 1024 * 1024


def _dot(a, b):
    return jnp.dot(a.astype(BF16), b.astype(BF16), preferred_element_type=F32)


def _dot_nt(a, b):
    return lax.dot_general(a.astype(BF16), b.astype(BF16), (((1,), (1,)), ((), ())),
                           preferred_element_type=F32)


def _split3(a):
    a1 = a.astype(BF16)
    r1 = a - a1.astype(F32)
    a2 = r1.astype(BF16)
    a3 = (r1 - a2.astype(F32)).astype(BF16)
    return a1, a2, a3


def _dot_exact_lhs(m, x):
    return sum(jnp.dot(m, p, preferred_element_type=F32) for p in _split3(x))


def _dot_exact_rhs(x, m):
    return sum(jnp.dot(p, m, preferred_element_type=F32) for p in _split3(x))


def _softplus(x):
    return jnp.maximum(x, 0.0) + jnp.log1p(jnp.exp(-jnp.abs(x)))


def _sigmoid(x):
    return 1.0 / (1.0 + jnp.exp(-x))


def _silu(x):
    return x * _sigmoid(x)


def _inproj_kernel(x_ref, nw_ref, wg_ref, wz_ref, wba_ref, wq_ref, wk_ref, wv_ref,
                   g_ref, z_ref, ba_ref, q_ref, k_ref, v_ref):
    x = x_ref[...]
    h = x * lax.rsqrt(jnp.mean(x * x, axis=-1, keepdims=True) + RMS_EPS)
    h = (h * nw_ref[...]).astype(BF16)
    for w_ref, o_ref in ((wg_ref, g_ref), (wz_ref, z_ref), (wba_ref, ba_ref),
                         (wq_ref, q_ref), (wk_ref, k_ref), (wv_ref, v_ref)):
        o_ref[...] = jnp.dot(h, w_ref[...], preferred_element_type=F32)


def _inproj(x, norm_w, w_in, tm=256):
    n, d = x.shape
    c1 = CONV_DIM
    c2 = c1 + GDN_WIDTH
    c4 = c2 + 2 * GDN_HEADS
    wb = w_in.astype(BF16)
    wba = jnp.pad(wb[:, c2:c4], ((0, 0), (0, LANES - 2 * GDN_HEADS)))
    ws = (wb[:, :c1], wb[:, c1:c2], wba, wb[:, c4:c4 + SB_WIDTH],
          wb[:, c4 + SB_WIDTH:c4 + 2 * SB_WIDTH], wb[:, c4 + 2 * SB_WIDTH:])
    full = lambda i: (0, 0)
    row = lambda i: (i, 0)
    return pl.pallas_call(
        _inproj_kernel,
        grid=(pl.cdiv(n, tm),),
        in_specs=[pl.BlockSpec((tm, d), row), pl.BlockSpec((1, d), full)]
        + [pl.BlockSpec(w.shape, full) for w in ws],
        out_specs=[pl.BlockSpec((tm, w.shape[1]), row) for w in ws],
        out_shape=[jax.ShapeDtypeStruct((n, w.shape[1]), F32) for w in ws],
        compiler_params=pltpu.CompilerParams(dimension_semantics=("arbitrary",),
                                             vmem_limit_bytes=VMEM_LIMIT),
        name="inproj",
    )(x, norm_w.reshape(1, d).astype(F32), *ws)


def _gdn_kernel(u_ref, z_ref, ba_ref, bat_ref, cbuf_ref, s0_ref, cw_ref, alog_l_ref, dtb_l_ref,
                alog_c_ref, dtb_c_ref, gn_ref, og_ref, sout_ref, cscr, sscr, *, valid):
    c = pl.program_id(1)
    ck = GDN_CHUNK
    halo = SUBLANES

    @pl.when(c == 0)
    def _():
        cscr[0:halo, :] = cbuf_ref[...]
        sscr[...] = s0_ref[...]

    cscr[halo:halo + ck, :] = u_ref[...]
    cw = cw_ref[...]
    base = halo - (CONV_WIDTH - 1)
    acc = cscr[pl.ds(base, ck), :] * cw[0:1, :]
    for i in range(1, CONV_WIDTH):
        acc = acc + cscr[pl.ds(base + i, ck), :] * cw[i:i + 1, :]
    xc = _silu(acc)
    cscr[0:halo, :] = cscr[ck:ck + halo, :]

    ba = ba_ref[...]
    bat = bat_ref[...]
    beta_c = _sigmoid(ba)
    g_c = -jnp.exp(alog_l_ref[...]) * _softplus(ba + dtb_l_ref[...])
    g_r = -jnp.exp(alog_c_ref[...]) * _softplus(bat + dtb_c_ref[...])
    if valid < ck:
        rows = lax.broadcasted_iota(jnp.int32, ba.shape, 0)
        beta_c = jnp.where(rows < valid, beta_c, 0.0)
        g_c = jnp.where(rows < valid, g_c, 0.0)
        cols = lax.broadcasted_iota(jnp.int32, bat.shape, 1)
        g_r = jnp.where(cols < valid, g_r, 0.0)

    ri = lax.broadcasted_iota(jnp.int32, (ck, ck), 0)
    ci = lax.broadcasted_iota(jnp.int32, (ck, ck), 1)
    incl = ri >= ci
    strict = ri > ci
    tri = incl.astype(BF16)
    triu = (ri <= ci).astype(BF16)
    eye_f = (ri == ci).astype(F32)
    eye_b = eye_f.astype(BF16)
    g_cum_c = _dot_exact_lhs(tri, g_c)
    g_cum_r = _dot_exact_rhs(g_r, triu)
    gn = gn_ref[...]
    s_in = [sscr[h] for h in range(GDN_HEADS)]
    z_all = z_ref[...]
    s_new = []
    outs = []

    for h in range(GDN_HEADS):
        hs = slice(h * HEAD_DIM, (h + 1) * HEAD_DIM)
        qh = xc[:, hs]
        kh = xc[:, GDN_WIDTH + h * HEAD_DIM:GDN_WIDTH + (h + 1) * HEAD_DIM]
        vh = xc[:, 2 * GDN_WIDTH + h * HEAD_DIM:2 * GDN_WIDTH + (h + 1) * HEAD_DIM]
        qh = qh * lax.rsqrt(jnp.sum(qh * qh, axis=-1, keepdims=True) + L2_EPS) * (HEAD_DIM ** -0.5)
        kh = kh * lax.rsqrt(jnp.sum(kh * kh, axis=-1, keepdims=True) + L2_EPS)
        b_c = beta_c[:, h:h + 1]
        gc = g_cum_c[:, GDN_HEADS + h:GDN_HEADS + h + 1]
        gr = g_cum_r[GDN_HEADS + h:GDN_HEADS + h + 1, :]
        gl = g_cum_c[ck - 1:ck, GDN_HEADS + h:GDN_HEADS + h + 1]
        decay = jnp.where(incl, jnp.exp(jnp.where(incl, gc - gr, 0.0)), 0.0)
        kb = kh * b_c
        kq = _dot_nt(jnp.concatenate([kb, qh], axis=0), kh)
        a = -jnp.where(strict, kq[:ck] * decay, 0.0)
        qk = kq[ck:] * decay
        r = eye_f + a
        p = _dot(a, a)
        n_sq = int(math.log2(ck)) - 1
        for lvl in range(n_sq):
            if lvl < n_sq - 1:
                rp = _dot(jnp.concatenate([r, p], axis=0), p)
                r = r + rp[:ck]
                p = rp[ck:]
            else:
                r = r + _dot(r, p)
        eg = jnp.exp(gc)
        u = _dot(r, vh * b_c)
        w = _dot(r, kb * eg)
        s = s_in[h]
        wq =_dot(jnp.concatenate([w, qh * eg], axis=0), s)
        v_new = u - wq[:ck]
        kt = kh * jnp.exp(gl - gc)
        kt_t = _dot_nt(eye_b, kt)
        m2 = _dot(jnp.concatenate([qk, kt_t], axis=0), v_new)
        o = wq[ck:] + m2[:ck]
        s_new.append(s * jnp.exp(gl) + m2[ck:])
        on = o * lax.rsqrt(jnp.mean(o * o, axis=-1, keepdims=True) + RMS_EPS) * gn
        outs.append(on * _silu(z_all[:, hs]))

    og_ref[...] = jnp.concatenate(outs, axis=-1).astype(og_ref.dtype)
    for h in range(GDN_HEADS):
        sscr[h] = s_new[h]
        sout_ref[h] = s_new[h]


def _gdn(u, z, ba, conv_buf, s0, conv_w, a_log, dt_bias, gdn_norm, valid):
    b, l, _ = u.shape
    ck = GDN_CHUNK
    nc = l // ck
    nh = GDN_HEADS
    bat = ba[:, :, :2 * nh].reshape(b, nc, ck, 2 * nh).transpose(0, 1, 3, 2)
    cbuf = jnp.pad(conv_buf.astype(F32), ((0, 0), (SUBLANES - (CONV_WIDTH - 1), 0), (0, 0)))
    cw = jnp.pad(conv_w.astype(F32), ((0, SUBLANES - CONV_WIDTH), (0, 0)))
    zeros = jnp.zeros((nh,), F32)
    alog_l = jnp.pad(jnp.concatenate([zeros, a_log.astype(F32)]), (0, LANES - 2 * nh)).reshape(1, LANES)
    dtb_l = jnp.pad(jnp.concatenate([zeros, dt_bias.astype(F32)]), (0, LANES - 2 * nh)).reshape(1, LANES)
    alog_c = jnp.concatenate([zeros, a_log.astype(F32)]).reshape(2 * nh, 1)
    dtb_c = jnp.concatenate([zeros, dt_bias.astype(F32)]).reshape(2 * nh, 1)
    gn = gdn_norm.astype(F32).reshape(1, HEAD_DIM)
    tok = lambda i, j: (i, j, 0)
    per_b = lambda i, j: (i, 0, 0)
    const = lambda i, j: (0, 0)
    return pl.pallas_call(
        functools.partial(_gdn_kernel, valid=valid),
        grid=(b, nc),
        in_specs=[
            pl.BlockSpec((None, ck, CONV_DIM), tok),
            pl.BlockSpec((None, ck, GDN_WIDTH), tok),
            pl.BlockSpec((None, ck, LANES), tok),
            pl.BlockSpec((None, None, 2 * nh, ck), lambda i, j: (i, j, 0, 0)),
            pl.BlockSpec((None, SUBLANES, CONV_DIM), per_b),
            pl.BlockSpec((None, nh, HEAD_DIM, HEAD_DIM), lambda i, j: (i, 0, 0, 0)),
            pl.BlockSpec((SUBLANES, CONV_DIM), const),
            pl.BlockSpec((1, LANES), const),
            pl.BlockSpec((1, LANES), const),
            pl.BlockSpec((2 * nh, 1), const),
            pl.BlockSpec((2 * nh, 1), const),
            pl.BlockSpec((1, HEAD_DIM), const),
        ],
        out_specs=[
            pl.BlockSpec((None, ck, GDN_WIDTH), tok),
            pl.BlockSpec((None, nh, HEAD_DIM, HEAD_DIM), lambda i, j: (i, 0, 0, 0)),
        ],
        out_shape=[jax.ShapeDtypeStruct((b, l, GDN_WIDTH), BF16),
                   jax.ShapeDtypeStruct((b, nh, HEAD_DIM, HEAD_DIM), F32)],
        scratch_shapes=[pltpu.VMEM((ck + SUBLANES, CONV_DIM), F32),
                        pltpu.VMEM((nh, HEAD_DIM, HEAD_DIM), F32)],
        compiler_params=pltpu.CompilerParams(dimension_semantics=("arbitrary", "arbitrary"),
                                             vmem_limit_bytes=VMEM_LIMIT),
        name="gdn",
    )(u, z, ba, bat, cbuf, s0.astype(F32), cw, alog_l, dtb_l, alog_c, dtb_c, gn)


def _sb_block(qh, k2, v2, bias, mask, ustrict, carry, acc):
    z = _dot_nt(qh, k2) + bias
    sp = _softplus(z)
    lr = jnp.where(mask, -sp, 0.0)
    lr_hi = lr.astype(BF16)
    lr_lo = (lr - lr_hi.astype(F32)).astype(BF16)
    excl = (jnp.dot(lr_hi, ustrict, preferred_element_type=F32)
            + jnp.dot(lr_lo, ustrict, preferred_element_type=F32))
    a = jnp.where(mask, jnp.exp(z - sp + carry + excl), 0.0)
    acc = acc + _dot(a, v2)
    carry = carry + excl[:, 0:1] + lr[:, 0:1]
    return carry, acc


def _sb_prompt_kernel(bias_ref, q_ref, k_ref, v_ref, o_ref, *, tq):
    p = pl.program_id(1)
    i = pl.program_id(2)
    tk = tq
    lane = lax.broadcasted_iota(jnp.int32, (tq, LANES), 1)
    q2 = q_ref[...] * (HEAD_DIM ** -0.5)
    qa = jnp.where(lane < HEAD_DIM, q2, 0.0).astype(BF16)
    qb = jnp.where(lane >= HEAD_DIM, q2, 0.0).astype(BF16)
    bias_a = bias_ref[2 * p]
    bias_b = bias_ref[2 * p + 1]
    row_id = i * tq + lax.broadcasted_iota(jnp.int32, (tq, 1), 0)
    ur = lax.broadcasted_iota(jnp.int32, (tk, tk), 0)
    uc = lax.broadcasted_iota(jnp.int32, (tk, tk), 1)
    ustrict = (ur > uc).astype(BF16)

    def body(jj, state):
        ca, cb, acca, accb = state
        j = i - jj
        start = pl.multiple_of(j * tk, tk)
        k2 = k_ref[pl.ds(start, tk), :].astype(BF16)
        v2 = v_ref[pl.ds(start, tk), :].astype(BF16)
        col_id = j * tk + lax.broadcasted_iota(jnp.int32, (1, tk), 1)
        mask = col_id < row_id
        ca, acca = _sb_block(qa, k2, v2, bias_a, mask, ustrict, ca, acca)
        cb, accb = _sb_block(qb, k2, v2, bias_b, mask, ustrict, cb, accb)
        return ca, cb, acca, accb

    zc = jnp.zeros((tq, 1), F32)
    za = jnp.zeros((tq, LANES), F32)
    _, _, acca, accb = lax.fori_loop(0, i + 1, body, (zc, zc, za, za))
    o_ref[...] = jnp.where(lane < HEAD_DIM, acca, accb).astype(o_ref.dtype)


def _sb_prompt(q, k, v, sb_bias, tq=256):
    b, l, w = q.shape
    assert l % tq == 0 and w % LANES == 0
    return pl.pallas_call(
        functools.partial(_sb_prompt_kernel, tq=tq),
        grid=(b, w // LANES, l // tq),
        in_specs=[
            pl.BlockSpec(memory_space=pltpu.SMEM),
            pl.BlockSpec((None, tq, LANES), lambda bi, p, i: (bi, i, p)),
            pl.BlockSpec((None, l, LANES), lambda bi, p, i: (bi, 0, p)),
            pl.BlockSpec((None, l, LANES), lambda bi, p, i: (bi, 0, p)),
        ],
        out_specs=pl.BlockSpec((None, tq, LANES), lambda bi, p, i: (bi, i, p)),
        out_shape=jax.ShapeDtypeStruct((b, l, w), BF16),
        compiler_params=pltpu.CompilerParams(
            dimension_semantics=("arbitrary", "arbitrary", "arbitrary"), vmem_limit_bytes=VMEM_LIMIT),
        name="sb_prompt",
    )(sb_bias.astype(F32), q, k, v)


def _sb_decode_kernel(pt_ref, qbd_ref, biasc_ref, knew_ref, vnew_ref, ka_ref, kb_ref, va_ref, vb_ref,
                      o_ref, carry_scr, acc_scr, *, n_new, n_steps):
    del pt_ref
    j = pl.program_id(1)
    nh = SB_HEADS
    rows = n_new * nh
    tk = knew_ref.shape[0]

    @pl.when(j == 0)
    def _():
        carry_scr[...] = jnp.zeros_like(carry_scr)
        acc_scr[...] = jnp.zeros_like(acc_scr)

    is_new = j == 0
    k_pages = jnp.concatenate([ka_ref[...], kb_ref[...]], axis=0)
    v_pages = jnp.concatenate([va_ref[...], vb_ref[...]], axis=0)
    kt = jnp.where(is_new, knew_ref[...], k_pages)
    vt = jnp.where(is_new, vnew_ref[...], v_pages)
    t_row = lax.broadcasted_iota(jnp.int32, (rows, 1), 0) // nh
    limit = jnp.where(is_new, t_row, tk)
    mask = lax.broadcasted_iota(jnp.int32, (1, tk), 1) < limit
    ur = lax.broadcasted_iota(jnp.int32, (tk, tk), 0)
    uc = lax.broadcasted_iota(jnp.int32, (tk, tk), 1)
    ustrict = (ur > uc).astype(BF16)
    carry, acc = _sb_block(qbd_ref[...], kt, vt, biasc_ref[...], mask, ustrict,
                           carry_scr[...], acc_scr[...])
    carry_scr[...] = carry
    acc_scr[...] = acc

    @pl.when(j == n_steps - 1)
    def _():
        hmask = (lax.broadcasted_iota(jnp.int32, (nh, SB_WIDTH), 1) // HEAD_DIM
                 == lax.broadcasted_iota(jnp.int32, (nh, SB_WIDTH), 0))
        for t in range(n_new):
            blk = jnp.where(hmask, acc[t * nh:(t + 1) * nh, :], 0.0)
            o_ref[t:t + 1, :] = jnp.sum(blk, axis=0, keepdims=True).astype(o_ref.dtype)


def _sb_decode(q, k_new, v_new, cache_k, cache_v, page_table, sb_bias):
    b, t, w = q.shape
    n_pool, page, _ = cache_k.shape
    n_pages = page_table.shape[1]
    nh = SB_HEADS
    assert n_pages % 2 == 0 and t <= 2 * page
    tk = 2 * page
    n_steps = 1 + n_pages // 2
    hmask = (jnp.arange(w)[None, :] // HEAD_DIM == jnp.arange(nh)[:, None]).astype(F32)
    qbd = (q.astype(F32)[:, :, None, :] * hmask[None, None] * (HEAD_DIM ** -0.5)).reshape(b, t * nh, w)
    biasc = jnp.tile(sb_bias.astype(F32), t).reshape(t * nh, 1)
    pad = lambda a: jnp.pad(a.astype(F32), ((0, 0), (0, tk - t), (0, 0)))
    first = lambda bi, j, pt: (pt[bi, n_pages - 2 * jnp.maximum(j, 1)], 0, 0)
    second = lambda bi, j, pt: (pt[bi, n_pages - 2 * jnp.maximum(j, 1) + 1], 0, 0)
    per_b = lambda bi, j, pt: (bi, 0, 0)
    grid_spec = pltpu.PrefetchScalarGridSpec(
        num_scalar_prefetch=1,
        grid=(b, n_steps),
        in_specs=[
            pl.BlockSpec((None, t * nh, w), per_b),
            pl.BlockSpec((t * nh, 1), lambda bi, j, pt: (0, 0)),
            pl.BlockSpec((None, tk, w), per_b),
            pl.BlockSpec((None, tk, w), per_b),
            pl.BlockSpec((None, page, w), first),
            pl.BlockSpec((None, page, w), second),
            pl.BlockSpec((None, page, w), first),
            pl.BlockSpec((None, page, w), second),
        ],
        out_specs=pl.BlockSpec((None, t, w), per_b),
        scratch_shapes=[pltpu.VMEM((t * nh, 1), F32), pltpu.VMEM((t * nh, w), F32)],
    )
    return pl.pallas_call(
        functools.partial(_sb_decode_kernel, n_new=t, n_steps=n_steps),
        grid_spec=grid_spec,
        out_shape=jax.ShapeDtypeStruct((b, t, w), F32),
        compiler_params=pltpu.CompilerParams(dimension_semantics=("arbitrary", "arbitrary"),
                                             vmem_limit_bytes=VMEM_LIMIT),
        name="sb_decode",
    )(page_table.astype(jnp.int32), qbd, biasc, pad(k_new), pad(v_new), cache_k, cache_k, cache_v, cache_v)


NEG_BIG = -1e30


def _outproj_router_kernel(x_ref, og_ref, os_ref, wo1_ref, wo2_ref, nw_ref, rw1_ref, rw2_ref, rb_ref,
                           x2_ref, xn_ref, ids_ref, gates_ref, cnt_ref):
    i = pl.program_id(0)
    x2 = (x_ref[...] + jnp.dot(og_ref[...], wo1_ref[...], preferred_element_type=F32)
          + jnp.dot(os_ref[...], wo2_ref[...], preferred_element_type=F32))
    x2_ref[...] = x2
    xn = x2 * lax.rsqrt(jnp.mean(x2 * x2, axis=-1, keepdims=True) + RMS_EPS) * nw_ref[...]
    xn_ref[...] = xn
    x_hi = xn.astype(BF16)
    x_lo = (xn - x_hi.astype(F32)).astype(BF16)
    logits = (jnp.dot(x_hi, rw1_ref[...], preferred_element_type=F32)
              + jnp.dot(x_hi, rw2_ref[...], preferred_element_type=F32)
              + jnp.dot(x_lo, rw1_ref[...], preferred_element_type=F32)) + rb_ref[...]
    tm = logits.shape[0]
    lane = lax.broadcasted_iota(jnp.int32, (tm, LANES), 1)
    lane_f = lane.astype(F32)
    vals, idxs = [], []
    cnt = jnp.zeros((tm, LANES), F32)
    l = logits
    for _ in range(TOP_K):
        m = jnp.max(l, axis=-1, keepdims=True)
        idx = jnp.min(jnp.where(l == m, lane_f, float(LANES)), axis=-1, keepdims=True)
        sel = lane_f == idx
        cnt = cnt + sel.astype(F32)
        l = jnp.where(sel, 2.0 * NEG_BIG, l)
        vals.append(m)
        idxs.append(idx)
    es = [jnp.exp(v - vals[0]) for v in vals]
    den = es[0]
    for e in es[1:]:
        den = den + e
    ids_out = jnp.zeros((tm, LANES), F32)
    gates_out = jnp.zeros((tm, LANES), F32)
    for k in range(TOP_K):
        ids_out = jnp.where(lane == k, idxs[k], ids_out)
        gates_out = jnp.where(lane == k, es[k] / den, gates_out)
    ids_ref[...] = ids_out.astype(jnp.int32)
    gates_ref[...] = gates_out

    @pl.when(i == 0)
    def _():
        cnt_ref[...] = jnp.zeros_like(cnt_ref)

    cnt_ref[...] += jnp.sum(cnt, axis=0, keepdims=True)


def _row_tile(n, candidates):
    for c in candidates:
        if n % c == 0:
            return c
    raise ValueError(f"no row tile among {candidates} divides {n}")


def _outproj_router(x, og, os_, w_out, norm_w, router_w, router_b):
    n, d = x.shape
    tm = _row_tile(n, (384, 256, 128, 64, 32, 16, 8))
    ne = router_w.shape[1]
    wo = w_out.astype(BF16)
    rw = jnp.pad(router_w.astype(F32), ((0, 0), (0, LANES - ne)))
    rw1 = rw.astype(BF16)
    rw2 = (rw - rw1.astype(F32)).astype(BF16)
    rb = jnp.pad(router_b.astype(F32), (0, LANES - ne), constant_values=NEG_BIG).reshape(1, LANES)
    row = lambda i: (i, 0)
    full = lambda i: (0, 0)
    gw = og.shape[1]
    return pl.pallas_call(
        _outproj_router_kernel,
        grid=(n // tm,),
        in_specs=[pl.BlockSpec((tm, d), row), pl.BlockSpec((tm, gw), row), pl.BlockSpec((tm, os_.shape[1]), row),
                  pl.BlockSpec((gw, d), full), pl.BlockSpec((os_.shape[1], d), full), pl.BlockSpec((1, d), full),
                  pl.BlockSpec((d, LANES), full), pl.BlockSpec((d, LANES), full), pl.BlockSpec((1, LANES), full)],
        out_specs=[pl.BlockSpec((tm, d), row), pl.BlockSpec((tm, d), row), pl.BlockSpec((tm, LANES), row),
                   pl.BlockSpec((tm, LANES), row), pl.BlockSpec((1, LANES), full)],
        out_shape=[jax.ShapeDtypeStruct((n, d), F32), jax.ShapeDtypeStruct((n, d), F32),
                   jax.ShapeDtypeStruct((n, LANES), jnp.int32), jax.ShapeDtypeStruct((n, LANES), F32),
                   jax.ShapeDtypeStruct((1, LANES), F32)],
        compiler_params=pltpu.CompilerParams(dimension_semantics=("arbitrary",), vmem_limit_bytes=VMEM_LIMIT),
        name="outproj_router",
    )(x, og, os_, wo[:gw], wo[gw:], norm_w.reshape(1, d).astype(F32), rw1, rw2, rb)


def _positions_kernel(ids_ref, base_ref, pos_ref, run_scr):
    i = pl.program_id(0)

    @pl.when(i == 0)
    def _():
        run_scr[...] = jnp.zeros_like(run_scr)

    tm = ids_ref.shape[0]
    ids = ids_ref[...].astype(F32)
    lane = lax.broadcasted_iota(jnp.int32, (tm, LANES), 1)
    lane_f = lane.astype(F32)
    r = lax.broadcasted_iota(jnp.int32, (tm, tm), 0)
    c = lax.broadcasted_iota(jnp.int32, (tm, tm), 1)
    below = (r > c).astype(BF16)
    offset = run_scr[...] + base_ref[...]
    pos_out = jnp.zeros((tm, LANES), F32)
    for k in range(TOP_K):
        oh = lane_f == ids[:, k:k + 1]
        ohf = oh.astype(F32)
        rank = jnp.dot(below, ohf.astype(BF16), preferred_element_type=F32) + offset
        pk = jnp.sum(jnp.where(oh, rank, 0.0), axis=-1, keepdims=True)
        pos_out = jnp.where(lane == k, pk, pos_out)
        offset = offset + jnp.sum(ohf, axis=0, keepdims=True)
    run_scr[...] = offset - base_ref[...]
    pos_ref[...] = pos_out.astype(jnp.int32)


def _positions(ids, base):
    n = ids.shape[0]
    tm = _row_tile(n, (384, 256, 128, 64, 32, 16, 8))
    return pl.pallas_call(
        _positions_kernel,
        grid=(n // tm,),
        in_specs=[pl.BlockSpec((tm, LANES), lambda i: (i, 0)), pl.BlockSpec((1, LANES), lambda i: (0, 0))],
        out_specs=pl.BlockSpec((tm, LANES), lambda i: (i, 0)),
        out_shape=jax.ShapeDtypeStruct((n, LANES), jnp.int32),
        scratch_shapes=[pltpu.VMEM((1, LANES), F32)],
        compiler_params=pltpu.CompilerParams(dimension_semantics=("arbitrary",)),
        name="positions",
    )(ids, base)


def _dispatch_kernel(pos_ref, xn_ref, zero_ref, xs_ref, sem):
    del zero_ref
    tm = xn_ref.shape[0]

    def row_copy(r, k):
        dst = pos_ref[0, r * TOP_K + k]
        return pltpu.make_async_copy(xn_ref.at[pl.ds(r, 1), :], xs_ref.at[pl.ds(dst, 1), :], sem)

    def body(r, carry):
        for k in range(TOP_K):
            row_copy(r, k).start()
        return carry

    lax.fori_loop(0, tm, body, 0)
    for _ in range(TOP_K):
        pltpu.make_async_copy(xn_ref, xs_ref.at[pl.ds(0, tm), :], sem).wait()


def _dispatch(xn, pos, n_slots):
    n, d = xn.shape
    tm = _row_tile(n, (384, 256, 128, 64, 32, 16, 8))
    pos2 = pos[:, :TOP_K].reshape(n // tm, 1, tm * TOP_K)
    zeros = jnp.zeros((n_slots, d), xn.dtype)
    return pl.pallas_call(
        _dispatch_kernel,
        grid=(n // tm,),
        in_specs=[pl.BlockSpec((None, 1, tm * TOP_K), lambda i: (i, 0, 0), memory_space=pltpu.SMEM),
                  pl.BlockSpec((tm, d), lambda i: (i, 0)),
                  pl.BlockSpec(memory_space=pl.ANY)],
        out_specs=pl.BlockSpec(memory_space=pl.ANY),
        out_shape=jax.ShapeDtypeStruct((n_slots, d), xn.dtype),
        scratch_shapes=[pltpu.SemaphoreType.DMA(())],
        input_output_aliases={2: 0},
        compiler_params=pltpu.CompilerParams(dimension_semantics=("arbitrary",)),
        name="dispatch",
    )(pos2, xn, zeros)


def _moe_kernel(be_ref, nu_ref, x_ref, wg_ref, wu_ref, wd_ref, bg_ref, bu_ref, bd_ref, y_ref):
    del be_ref
    b = pl.program_id(0)

    @pl.when(b < nu_ref[0])
    def _():
        x = x_ref[...].astype(BF16)
        g = jnp.dot(x, wg_ref[...], preferred_element_type=F32) + bg_ref[...]
        u = jnp.dot(x, wu_ref[...], preferred_element_type=F32) + bu_ref[...]
        g = jnp.minimum(g, SWIGLU_LIMIT)
        u = jnp.clip(u, -SWIGLU_LIMIT, SWIGLU_LIMIT)
        act = (u + 1.0) * g * _sigmoid(SWIGLU_ALPHA * g)
        y_ref[...] = jnp.dot(act.astype(BF16), wd_ref[...], preferred_element_type=F32) + bd_ref[...]

    @pl.when(b >= nu_ref[0])
    def _():
        y_ref[...] = jnp.zeros_like(y_ref)


def _moe_mm(xs, block_e, n_used, wg, wu, wd, bg, bu, bd, bm):
    n_slots, d = xs.shape
    ff = wg.shape[2]
    n_blocks = n_slots // bm
    wsel = lambda b, be, nu: (be[b], 0, 0)
    grid_spec = pltpu.PrefetchScalarGridSpec(
        num_scalar_prefetch=2,
        grid=(n_blocks,),
        in_specs=[pl.BlockSpec((bm, d), lambda b, be, nu: (b, 0)),
                  pl.BlockSpec((None, d, ff), wsel), pl.BlockSpec((None, d, ff), wsel),
                  pl.BlockSpec((None, ff, d), wsel),
                  pl.BlockSpec((None, 1, ff), wsel), pl.BlockSpec((None, 1, ff), wsel),
                  pl.BlockSpec((None, 1, d), wsel)],
        out_specs=pl.BlockSpec((bm, d), lambda b, be, nu: (b, 0)),
    )
    return pl.pallas_call(
        _moe_kernel,
        grid_spec=grid_spec,
        out_shape=jax.ShapeDtypeStruct((n_slots, d), F32),
        compiler_params=pltpu.CompilerParams(dimension_semantics=("arbitrary",), vmem_limit_bytes=VMEM_LIMIT),
        name="moe_mm",
    )(block_e, n_used, xs, wg, wu, wd, bg, bu, bd)


def _combine_kernel(pos_ref, x2_ref, gates_ref, nw_ref, ys_ref, y_ref, buf, sem):
    tm = x2_ref.shape[0]

    def row_copy(r, k):
        src = pos_ref[0, r * TOP_K + k]
        return pltpu.make_async_copy(ys_ref.at[pl.ds(src, 1), :], buf.at[k, pl.ds(r, 1), :], sem)

    def body(r, carry):
        for k in range(TOP_K):
            row_copy(r, k).start()
        return carry

    lax.fori_loop(0, tm, body, 0)
    for k in range(TOP_K):
        pltpu.make_async_copy(ys_ref.at[pl.ds(0, tm), :], buf.at[k], sem).wait()
    gates = gates_ref[...]
    y = x2_ref[...]
    for k in range(TOP_K):
        y = y + buf[k] * gates[:, k:k + 1]
    y_ref[...] = y * lax.rsqrt(jnp.mean(y * y, axis=-1, keepdims=True) + RMS_EPS) * nw_ref[...]


def _combine(ys, pos, gates, x2, norm_w):
    n, d = x2.shape
    tm = _row_tile(n, (384, 256, 128, 64, 32, 16, 8))
    pos2 = pos[:, :TOP_K].reshape(n // tm, 1, tm * TOP_K)
    row = lambda i: (i, 0)
    return pl.pallas_call(
        _combine_kernel,
        grid=(n // tm,),
        in_specs=[pl.BlockSpec((None, 1, tm * TOP_K), lambda i: (i, 0, 0), memory_space=pltpu.SMEM),
                  pl.BlockSpec((tm, d), row), pl.BlockSpec((tm, LANES), row),
                  pl.BlockSpec((1, d), lambda i: (0, 0)), pl.BlockSpec(memory_space=pl.ANY)],
        out_specs=pl.BlockSpec((tm, d), row),
        out_shape=jax.ShapeDtypeStruct((n, d), F32),
        scratch_shapes=[pltpu.VMEM((TOP_K, tm, d), F32), pltpu.SemaphoreType.DMA(())],
        compiler_params=pltpu.CompilerParams(dimension_semantics=("arbitrary",), vmem_limit_bytes=VMEM_LIMIT),
        name="combine",
    )(pos2, x2, gates, norm_w.reshape(1, d).astype(F32), ys)


def _moe(x2, xn, ids, gates, counts, w_up, b_up, w_down, b_down, norm_final, bm=256):
    n, d = x2.shape
    ne = w_up.shape[0]
    n_pairs = n * TOP_K
    n_blocks = -(-n_pairs // bm) + ne
    n_slots = n_blocks * bm
    cnt = counts[0, :ne].astype(jnp.int32)
    padded = (cnt + bm - 1) // bm * bm
    ends = jnp.cumsum(padded)
    base = jnp.pad((ends - padded).astype(F32), (0, LANES - ne)).reshape(1, LANES)
    block_e = jnp.minimum(jnp.searchsorted(ends, jnp.arange(n_blocks, dtype=jnp.int32) * bm, side="right"),
                          ne - 1).astype(jnp.int32)
    n_used = (ends[-1:] // bm).astype(jnp.int32)
    pos = _positions(ids, base)
    xs = _dispatch(xn, pos, n_slots)
    wg = w_up[:, :, 0::2].astype(BF16)
    wu = w_up[:, :, 1::2].astype(BF16)
    wd = w_down.astype(BF16)
    bg = b_up[:, 0::2].astype(F32)[:, None, :]
    bu = b_up[:, 1::2].astype(F32)[:, None, :]
    bd = b_down.astype(F32)[:, None, :]
    ys = _moe_mm(xs, block_e, n_used, wg, wu, wd, bg, bu, bd, bm)
    return _combine(ys, pos, gates, x2, norm_final)


def kernel(x_prompt, x_sample, cache_k, cache_v, page_table, state_conv, state_ssm, norm_mix, w_in, conv_w, a_log, dt_bias, gdn_norm, sb_bias, w_out, norm_ffn, router_w, router_b, w_up, b_up, w_down, b_down, norm_final):
    assert w_in.shape[0] == 1, "single-layer trunk"
    bp, lp, d = x_prompt.shape
    bs, ls, _ = x_sample.shape
    assert lp % GDN_CHUNK == 0 and CONV_WIDTH - 1 <= ls <= GDN_CHUNK
    n_p = bp * lp
    x_all = jnp.concatenate([x_prompt.reshape(n_p, d), x_sample.reshape(bs * ls, d)], axis=0)
    u, z, ba, q_s, k_s, v_s = _inproj(x_all, norm_mix[0], w_in[0])

    up, zp, bap, qp, kp, vp = (t[:n_p].reshape(bp, lp, -1) for t in (u, z, ba, q_s, k_s, v_s))
    og_p, ssm_p = _gdn(up, zp, bap, jnp.zeros((bp, CONV_WIDTH - 1, CONV_DIM), F32),
                       jnp.zeros((bp, GDN_HEADS, HEAD_DIM, HEAD_DIM), F32),
                       conv_w[0], a_log[0], dt_bias[0], gdn_norm[0], valid=GDN_CHUNK)
    os_p = _sb_prompt(qp, kp, vp, sb_bias[0])

    us, zs, bas, qs, ks, vs = (t[n_p:].reshape(bs, ls, -1) for t in (u, z, ba, q_s, k_s, v_s))
    padl = lambda t: jnp.pad(t, ((0, 0), (0, GDN_CHUNK - ls), (0, 0)))
    og_s, ssm_s = _gdn(padl(us), padl(zs), padl(bas), state_conv[0], state_ssm[0],
                       conv_w[0], a_log[0], dt_bias[0], gdn_norm[0], valid=ls)
    n_pool, page = cache_k.shape[1], cache_k.shape[2]
    os_s = _sb_decode(qs, ks, vs, cache_k[0].reshape(n_pool, page, SB_WIDTH),
                      cache_v[0].reshape(n_pool, page, SB_WIDTH), page_table, sb_bias[0])

    og = jnp.concatenate([og_p.reshape(n_p, GDN_WIDTH), og_s[:, :ls].reshape(bs * ls, GDN_WIDTH)], axis=0)
    os_ = jnp.concatenate([os_p.reshape(n_p, SB_WIDTH), os_s.reshape(bs * ls, SB_WIDTH).astype(BF16)], axis=0)
    x2, xn, ids, gates, counts = _outproj_router(x_all, og, os_, w_out[0], norm_ffn[0], router_w[0], router_b[0])
    y = _moe(x2, xn, ids, gates, counts, w_up[0], b_up[0], w_down[0], b_down[0], norm_final)

    heads = lambda t, b, l: t.reshape(1, b, l, SB_HEADS, HEAD_DIM)
    tail = CONV_WIDTH - 1
    return (y[:n_p].reshape(bp, lp, d), y[n_p:].reshape(bs, ls, d),
            heads(kp, bp, lp), heads(vp, bp, lp), up[None, :, lp - tail:, :], ssm_p[None],
            heads(ks, bs, ls), heads(vs, bs, ls), us[None, :, ls - tail:, :], ssm_s[None])
```

```python
import functools
import math

import jax
import jax.numpy as jnp
from jax import lax
from jax.experimental import pallas as pl
from jax.experimental.pallas import tpu as pltpu

F32 = jnp.float32
BF16 = jnp.bfloat16

HEAD_DIM = 64
GDN_HEADS = 8
SB_HEADS = 8
GDN_WIDTH = GDN_HEADS * HEAD_DIM
SB_WIDTH = SB_HEADS * HEAD_DIM
CONV_WIDTH = 4
CONV_DIM = 3 * GDN_WIDTH
GDN_CHUNK = 64
TOP_K = 4
SWIGLU_LIMIT = 7.0
SWIGLU_ALPHA = 1.702
RMS_EPS = 1e-6
L2_EPS = 1e-6
LANES = 128
SUBLANES = 8
VMEM_LIMIT = 56 * 1024 * 1024


def _dot(a, b):
    return jnp.dot(a.astype(BF16), b.astype(BF16), preferred_element_type=F32)


def _dot_nt(a, b):
    return lax.dot_general(a.astype(BF16), b.astype(BF16), (((1,), (1,)), ((), ())),
                           preferred_element_type=F32)


def _split3(a):
    a1 = a.astype(BF16)
    r1 = a - a1.astype(F32)
    a2 = r1.astype(BF16)
    a3 = (r1 - a2.astype(F32)).astype(BF16)
    return a1, a2, a3


def _dot_exact_lhs(m, x):
    return sum(jnp.dot(m, p, preferred_element_type=F32) for p in _split3(x))


def _dot_exact_rhs(x, m):
    return sum(jnp.dot(p, m, preferred_element_type=F32) for p in _split3(x))


def _softplus(x):
    return jnp.maximum(x, 0.0) + jnp.log1p(jnp.exp(-jnp.abs(x)))


def _sigmoid(x):
    return 1.0 / (1.0 + jnp.exp(-x))


def _silu(x):
    return x * _sigmoid(x)


def _inproj_kernel(x_ref, nw_ref, wg_ref, wz_ref, wba_ref, wq_ref, wk_ref, wv_ref,
                   g_ref, z_ref, ba_ref, q_ref, k_ref, v_ref):
    x = x_ref[...]
    h = x * lax.rsqrt(jnp.mean(x * x, axis=-1, keepdims=True) + RMS_EPS)
    h = (h * nw_ref[...]).astype(BF16)
    for w_ref, o_ref in ((wg_ref, g_ref), (wz_ref, z_ref), (wba_ref, ba_ref),
                         (wq_ref, q_ref), (wk_ref, k_ref), (wv_ref, v_ref)):
        o_ref[...] = jnp.dot(h, w_ref[...], preferred_element_type=F32)


def _inproj(x, norm_w, w_in, tm=256):
    n, d = x.shape
    c1 = CONV_DIM
    c2 = c1 + GDN_WIDTH
    c4 = c2 + 2 * GDN_HEADS
    wb = w_in.astype(BF16)
    wba = jnp.pad(wb[:, c2:c4], ((0, 0), (0, LANES - 2 * GDN_HEADS)))
    ws = (wb[:, :c1], wb[:, c1:c2], wba, wb[:, c4:c4 + SB_WIDTH],
          wb[:, c4 + SB_WIDTH:c4 + 2 * SB_WIDTH], wb[:, c4 + 2 * SB_WIDTH:])
    full = lambda i: (0, 0)
    row = lambda i: (i, 0)
    return pl.pallas_call(
        _inproj_kernel,
        grid=(pl.cdiv(n, tm),),
        in_specs=[pl.BlockSpec((tm, d), row), pl.BlockSpec((1, d), full)]
        + [pl.BlockSpec(w.shape, full) for w in ws],
        out_specs=[pl.BlockSpec((tm, w.shape[1]), row) for w in ws],
        out_shape=[jax.ShapeDtypeStruct((n, w.shape[1]), F32) for w in ws],
        compiler_params=pltpu.CompilerParams(dimension_semantics=("arbitrary",),
                                             vmem_limit_bytes=VMEM_LIMIT),
        name="inproj",
    )(x, norm_w.reshape(1, d).astype(F32), *ws)


def _gdn_kernel(u_ref, z_ref, ba_ref, bat_ref, cbuf_ref, s0_ref, cw_ref, alog_l_ref, dtb_l_ref,
                alog_c_ref, dtb_c_ref, gn_ref, og_ref, sout_ref, cscr, sscr, *, valid):
    c = pl.program_id(1)
    ck = GDN_CHUNK
    halo = SUBLANES

    @pl.when(c == 0)
    def _():
        cscr[0:halo, :] = cbuf_ref[...]
        sscr[...] = s0_ref[...]

    cscr[halo:halo + ck, :] = u_ref[...]
    cw = cw_ref[...]
    base = halo - (CONV_WIDTH - 1)
    acc = cscr[pl.ds(base, ck), :] * cw[0:1, :]
    for i in range(1, CONV_WIDTH):
        acc = acc + cscr[pl.ds(base + i, ck), :] * cw[i:i + 1, :]
    xc = _silu(acc)
    cscr[0:halo, :] = cscr[ck:ck + halo, :]

    ba = ba_ref[...]
    bat = bat_ref[...]
    beta_c = _sigmoid(ba)
    g_c = -jnp.exp(alog_l_ref[...]) * _softplus(ba + dtb_l_ref[...])
    g_r = -jnp.exp(alog_c_ref[...]) * _softplus(bat + dtb_c_ref[...])
    if valid < ck:
        rows = lax.broadcasted_iota(jnp.int32, ba.shape, 0)
        beta_c = jnp.where(rows < valid, beta_c, 0.0)
        g_c = jnp.where(rows < valid, g_c, 0.0)
        cols = lax.broadcasted_iota(jnp.int32, bat.shape, 1)
        g_r = jnp.where(cols < valid, g_r, 0.0)

    ri = lax.broadcasted_iota(jnp.int32, (ck, ck), 0)
    ci = lax.broadcasted_iota(jnp.int32, (ck, ck), 1)
    incl = ri >= ci
    strict = ri > ci
    tri = incl.astype(BF16)
    triu = (ri <= ci).astype(BF16)
    eye_f = (ri == ci).astype(F32)
    eye_b = eye_f.astype(BF16)
    g_cum_c = _dot_exact_lhs(tri, g_c)
    g_cum_r = _dot_exact_rhs(g_r, triu)
    gn = gn_ref[...]
    s_in = [sscr[h] for h in range(GDN_HEADS)]
    z_all = z_ref[...]
    s_new = []
    outs = []

    for h in range(GDN_HEADS):
        hs = slice(h * HEAD_DIM, (h + 1) * HEAD_DIM)
        qh = xc[:, hs]
        kh = xc[:, GDN_WIDTH + h * HEAD_DIM:GDN_WIDTH + (h + 1) * HEAD_DIM]
        vh = xc[:, 2 * GDN_WIDTH + h * HEAD_DIM:2 * GDN_WIDTH + (h + 1) * HEAD_DIM]
        qh = qh * lax.rsqrt(jnp.sum(qh * qh, axis=-1, keepdims=True) + L2_EPS) * (HEAD_DIM ** -0.5)
        kh = kh * lax.rsqrt(jnp.sum(kh * kh, axis=-1, keepdims=True) + L2_EPS)
        b_c = beta_c[:, h:h + 1]
        gc = g_cum_c[:, GDN_HEADS + h:GDN_HEADS + h + 1]
        gr = g_cum_r[GDN_HEADS + h:GDN_HEADS + h + 1, :]
        gl = g_cum_c[ck - 1:ck, GDN_HEADS + h:GDN_HEADS + h + 1]
        decay = jnp.where(incl, jnp.exp(jnp.where(incl, gc - gr, 0.0)), 0.0)
        kb = kh * b_c
        kq = _dot_nt(jnp.concatenate([kb, qh], axis=0), kh)
        a = -jnp.where(strict, kq[:ck] * decay, 0.0)
        qk = kq[ck:] * decay
        r = eye_f + a
        p = _dot(a, a)
        n_sq = int(math.log2(ck)) - 1
        for lvl in range(n_sq):
            if lvl < n_sq - 1:
                rp = _dot(jnp.concatenate([r, p], axis=0), p)
                r = r + rp[:ck]
                p = rp[ck:]
            else:
                r = r + _dot(r, p)
        eg = jnp.exp(gc)
        u = _dot(r, vh * b_c)
        w = _dot(r, kb * eg)
        s = s_in[h]
        wq =_dot(jnp.concatenate([w, qh * eg], axis=0), s)
        v_new = u - wq[:ck]
        kt = kh * jnp.exp(gl - gc)
        kt_t = _dot_nt(eye_b, kt)
        m2 = _dot(jnp.concatenate([qk, kt_t], axis=0), v_new)
        o = wq[ck:] + m2[:ck]
        s_new.append(s * jnp.exp(gl) + m2[ck:])
        on = o * lax.rsqrt(jnp.mean(o * o, axis=-1, keepdims=True) + RMS_EPS) * gn
        outs.append(on * _silu(z_all[:, hs]))

    og_ref[...] = jnp.concatenate(outs, axis=-1).astype(og_ref.dtype)
    for h in range(GDN_HEADS):
        sscr[h] = s_new[h]
        sout_ref[h] = s_new[h]


def _gdn(u, z, ba, conv_buf, s0, conv_w, a_log, dt_bias, gdn_norm, valid):
    b, l, _ = u.shape
    ck = GDN_CHUNK
    nc = l // ck
    nh = GDN_HEADS
    bat = ba[:, :, :2 * nh].reshape(b, nc, ck, 2 * nh).transpose(0, 1, 3, 2)
    cbuf = jnp.pad(conv_buf.astype(F32), ((0, 0), (SUBLANES - (CONV_WIDTH - 1), 0), (0, 0)))
    cw = jnp.pad(conv_w.astype(F32), ((0, SUBLANES - CONV_WIDTH), (0, 0)))
    zeros = jnp.zeros((nh,), F32)
    alog_l = jnp.pad(jnp.concatenate([zeros, a_log.astype(F32)]), (0, LANES - 2 * nh)).reshape(1, LANES)
    dtb_l = jnp.pad(jnp.concatenate([zeros, dt_bias.astype(F32)]), (0, LANES - 2 * nh)).reshape(1, LANES)
    alog_c = jnp.concatenate([zeros, a_log.astype(F32)]).reshape(2 * nh, 1)
    dtb_c = jnp.concatenate([zeros, dt_bias.astype(F32)]).reshape(2 * nh, 1)
    gn = gdn_norm.astype(F32).reshape(1, HEAD_DIM)
    tok = lambda i, j: (i, j, 0)
    per_b = lambda i, j: (i, 0, 0)
    const = lambda i, j: (0, 0)
    return pl.pallas_call(
        functools.partial(_gdn_kernel, valid=valid),
        grid=(b, nc),
        in_specs=[
            pl.BlockSpec((None, ck, CONV_DIM), tok),
            pl.BlockSpec((None, ck, GDN_WIDTH), tok),
            pl.BlockSpec((None, ck, LANES), tok),
            pl.BlockSpec((None, None, 2 * nh, ck), lambda i, j: (i, j, 0, 0)),
            pl.BlockSpec((None, SUBLANES, CONV_DIM), per_b),
            pl.BlockSpec((None, nh, HEAD_DIM, HEAD_DIM), lambda i, j: (i, 0, 0, 0)),
            pl.BlockSpec((SUBLANES, CONV_DIM), const),
            pl.BlockSpec((1, LANES), const),
            pl.BlockSpec((1, LANES), const),
            pl.BlockSpec((2 * nh, 1), const),
            pl.BlockSpec((2 * nh, 1), const),
            pl.BlockSpec((1, HEAD_DIM), const),
        ],
        out_specs=[
            pl.BlockSpec((None, ck, GDN_WIDTH), tok),
            pl.BlockSpec((None, nh, HEAD_DIM, HEAD_DIM), lambda i, j: (i, 0, 0, 0)),
        ],
        out_shape=[jax.ShapeDtypeStruct((b, l, GDN_WIDTH), BF16),
                   jax.ShapeDtypeStruct((b, nh, HEAD_DIM, HEAD_DIM), F32)],
        scratch_shapes=[pltpu.VMEM((ck + SUBLANES, CONV_DIM), F32),
                        pltpu.VMEM((nh, HEAD_DIM, HEAD_DIM), F32)],
        compiler_params=pltpu.CompilerParams(dimension_semantics=("arbitrary", "arbitrary"),
                                             vmem_limit_bytes=VMEM_LIMIT),
        name="gdn",
    )(u, z, ba, bat, cbuf, s0.astype(F32), cw, alog_l, dtb_l, alog_c, dtb_c, gn)


def _bdot(a, b):
    return lax.dot_general(a.astype(BF16), b.astype(BF16), (((2,), (1,)), ((0,), (0,))),
                           preferred_element_type=F32)


def _bdot_nt(a, b):
    return lax.dot_general(a.astype(BF16), b.astype(BF16), (((2,), (2,)), ((0,), (0,))),
                           preferred_element_type=F32)


def _gdnb_kernel(u_ref, z_ref, ba_ref, bat_ref, cbuf_ref, s0_ref, cw_ref, alog_l_ref, dtb_l_ref,
                 alog_c_ref, dtb_c_ref, gn_ref, og_ref, sout_ref, cscr, sscr, *, valid, nch):
    c0 = pl.program_id(1)
    ck = GDN_CHUNK
    nh = GDN_HEADS
    tt = nch * ck
    halo = SUBLANES

    @pl.when(c0 == 0)
    def _():
        cscr[0:halo, :] = cbuf_ref[...]
        sscr[...] = s0_ref[...]

    cscr[halo:halo + tt, :] = u_ref[...]
    cw = cw_ref[...]
    base = halo - (CONV_WIDTH - 1)
    acc = cscr[pl.ds(base, tt), :] * cw[0:1, :]
    for i in range(1, CONV_WIDTH):
        acc = acc + cscr[pl.ds(base + i, tt), :] * cw[i:i + 1, :]
    xc = _silu(acc)
    cscr[0:halo, :] = cscr[tt:tt + halo, :]

    ba = ba_ref[...]
    bat = bat_ref[...]
    beta_c = _sigmoid(ba)
    g_c = -jnp.exp(alog_l_ref[...]) * _softplus(ba + dtb_l_ref[...])
    g_r = -jnp.exp(alog_c_ref[...]) * _softplus(bat + dtb_c_ref[...])
    if valid < ck:
        rows = lax.broadcasted_iota(jnp.int32, ba.shape, 0)
        beta_c = jnp.where(rows < valid, beta_c, 0.0)
        g_c = jnp.where(rows < valid, g_c, 0.0)
        cols = lax.broadcasted_iota(jnp.int32, bat.shape, 2)
        g_r = jnp.where(cols < valid, g_r, 0.0)

    ri = lax.broadcasted_iota(jnp.int32, (ck, ck), 0)
    ci = lax.broadcasted_iota(jnp.int32, (ck, ck), 1)
    incl = (ri >= ci)[None]
    strict = (ri > ci)[None]
    tri = (ri >= ci).astype(BF16)
    triu = (ri <= ci).astype(BF16)
    eye_f = (ri == ci).astype(F32)[None]
    pairs = [(c, h) for c in range(nch) for h in range(nh)]
    n = len(pairs)
    g_cum_c = [_dot_exact_lhs(tri, g_c[c * ck:(c + 1) * ck]) for c in range(nch)]
    g_cum_r = [_dot_exact_rhs(g_r[c], triu) for c in range(nch)]

    def heads(x2d, off):
        return jnp.stack([x2d[c * ck:(c + 1) * ck, off + h * HEAD_DIM:off + (h + 1) * HEAD_DIM]
                          for c, h in pairs], axis=0)

    q = heads(xc, 0)
    k = heads(xc, GDN_WIDTH)
    v = heads(xc, 2 * GDN_WIDTH)
    zg = heads(z_ref[...], 0)
    q = q * lax.rsqrt(jnp.sum(q * q, axis=-1, keepdims=True) + L2_EPS) * (HEAD_DIM ** -0.5)
    k = k * lax.rsqrt(jnp.sum(k * k, axis=-1, keepdims=True) + L2_EPS)
    b_col = jnp.stack([beta_c[c * ck:(c + 1) * ck, h:h + 1] for c, h in pairs], axis=0)
    gc = jnp.stack([g_cum_c[c][:, nh + h:nh + h + 1] for c, h in pairs], axis=0)
    gr = jnp.stack([g_cum_r[c][nh + h:nh + h + 1, :] for c, h in pairs], axis=0)
    gl = jnp.stack([g_cum_c[c][ck - 1:ck, nh + h:nh + h + 1] for c, h in pairs], axis=0)
    decay = jnp.where(incl, jnp.exp(jnp.where(incl, gc - gr, 0.0)), 0.0)
    kb = k * b_col
    kq = _bdot_nt(jnp.concatenate([kb, q], axis=1), k)
    a = -jnp.where(strict, kq[:, :ck] * decay, 0.0)
    qk = kq[:, ck:] * decay
    r = eye_f + a
    p = _bdot(a, a)
    n_sq = int(math.log2(ck)) - 1
    for lvl in range(n_sq):
        if lvl < n_sq - 1:
            rp = _bdot(jnp.concatenate([r, p], axis=1), p)
            r = r + rp[:, :ck]
            p = rp[:, ck:]
        else:
            r = r + _bdot(r, p)
    eg = jnp.exp(gc)
    u = _bdot(r, v * b_col)
    w = _bdot(r, kb * eg)
    kt = k * jnp.exp(gl - gc)
    kt_t = _bdot_nt(jnp.broadcast_to(eye_f, (n, ck, ck)), kt)
    wq_lhs = jnp.concatenate([w, q * eg], axis=1).astype(BF16)
    m_lhs = jnp.concatenate([qk, kt_t], axis=1).astype(BF16)
    egl = jnp.exp(gl)
    gn = gn_ref[...]

    s = sscr[...]
    for c in range(nch):
        sl = slice(c * nh, (c + 1) * nh)
        wq = _bdot(wq_lhs[sl], s)
        v_new = u[sl] - wq[:, :ck]
        m2 = _bdot(m_lhs[sl], v_new)
        o = wq[:, ck:] + m2[:, :ck]
        s = s * egl[sl] + m2[:, ck:]
        on = o * lax.rsqrt(jnp.mean(o * o, axis=-1, keepdims=True) + RMS_EPS) * gn
        og_ref[:, c * ck:(c + 1) * ck, :] = (on * _silu(zg[sl])).astype(og_ref.dtype)
    sscr[...] = s
    sout_ref[...] = s


def _gdnb(u, z, ba, conv_buf, s0, conv_w, a_log, dt_bias, gdn_norm, valid):
    b, l, _ = u.shape
    ck = GDN_CHUNK
    nc = l // ck
    nch = math.gcd(nc, 4)
    tt = nch * ck
    nh = GDN_HEADS
    bat = ba[:, :, :2 * nh].reshape(b, nc, ck, 2 * nh).transpose(0, 1, 3, 2)
    cbuf = jnp.pad(conv_buf.astype(F32), ((0, 0), (SUBLANES - (CONV_WIDTH - 1), 0), (0, 0)))
    cw = jnp.pad(conv_w.astype(F32), ((0, SUBLANES - CONV_WIDTH), (0, 0)))
    zeros = jnp.zeros((nh,), F32)
    alog_l = jnp.pad(jnp.concatenate([zeros, a_log.astype(F32)]), (0, LANES - 2 * nh)).reshape(1, LANES)
    dtb_l = jnp.pad(jnp.concatenate([zeros, dt_bias.astype(F32)]), (0, LANES - 2 * nh)).reshape(1, LANES)
    alog_c = jnp.concatenate([zeros, a_log.astype(F32)]).reshape(2 * nh, 1)
    dtb_c = jnp.concatenate([zeros, dt_bias.astype(F32)]).reshape(2 * nh, 1)
    gn = gdn_norm.astype(F32).reshape(1, HEAD_DIM)
    tok = lambda i, j: (i, j, 0)
    per_b = lambda i, j: (i, 0, 0)
    const = lambda i, j: (0, 0)
    og, s_out = pl.pallas_call(
        functools.partial(_gdnb_kernel, valid=valid, nch=nch),
        grid=(b, nc // nch),
        in_specs=[
            pl.BlockSpec((None, tt, CONV_DIM), tok),
            pl.BlockSpec((None, tt, GDN_WIDTH), tok),
            pl.BlockSpec((None, tt, LANES), tok),
            pl.BlockSpec((None, nch, 2 * nh, ck), lambda i, j: (i, j, 0, 0)),
            pl.BlockSpec((None, SUBLANES, CONV_DIM), per_b),
            pl.BlockSpec((None, nh, HEAD_DIM, HEAD_DIM), lambda i, j: (i, 0, 0, 0)),
            pl.BlockSpec((SUBLANES, CONV_DIM), const),
            pl.BlockSpec((1, LANES), const),
            pl.BlockSpec((1, LANES), const),
            pl.BlockSpec((2 * nh, 1), const),
            pl.BlockSpec((2 * nh, 1), const),
            pl.BlockSpec((1, HEAD_DIM), const),
        ],
        out_specs=[
            pl.BlockSpec((None, nh, tt, HEAD_DIM), lambda i, j: (i, 0, j, 0)),
            pl.BlockSpec((None, nh, HEAD_DIM, HEAD_DIM), lambda i, j: (i, 0, 0, 0)),
        ],
        out_shape=[jax.ShapeDtypeStruct((b, nh, l, HEAD_DIM), BF16),
                   jax.ShapeDtypeStruct((b, nh, HEAD_DIM, HEAD_DIM), F32)],
        scratch_shapes=[pltpu.VMEM((tt + SUBLANES, CONV_DIM), F32),
                        pltpu.VMEM((nh, HEAD_DIM, HEAD_DIM), F32)],
        compiler_params=pltpu.CompilerParams(dimension_semantics=("arbitrary", "arbitrary"),
                                             vmem_limit_bytes=VMEM_LIMIT),
        name="gdn",
    )(u, z, ba, bat, cbuf, s0.astype(F32), cw, alog_l, dtb_l, alog_c, dtb_c, gn)
    return og.transpose(0, 2, 1, 3).reshape(b, l, GDN_WIDTH), s_out


def _sb_block(qh, k2, v2, bias, mask, ustrict, carry, acc):
    z = _dot_nt(qh, k2) + bias
    sp = _softplus(z)
    lr = jnp.where(mask, -sp, 0.0)
    lr_hi = lr.astype(BF16)
    lr_lo = (lr - lr_hi.astype(F32)).astype(BF16)
    excl = (jnp.dot(lr_hi, ustrict, preferred_element_type=F32)
            + jnp.dot(lr_lo, ustrict, preferred_element_type=F32))
    a = jnp.where(mask, jnp.exp(z - sp + carry + excl), 0.0)
    acc = acc + _dot(a, v2)
    carry = carry + excl[:, 0:1] + lr[:, 0:1]
    return carry, acc


def _sb_prompt_kernel(bias_ref, q_ref, k_ref, v_ref, o_ref, *, tq):
    p = pl.program_id(1)
    i = pl.program_id(2)
    tk = tq
    lane = lax.broadcasted_iota(jnp.int32, (tq, LANES), 1)
    q2 = q_ref[...] * (HEAD_DIM ** -0.5)
    qa = jnp.where(lane < HEAD_DIM, q2, 0.0).astype(BF16)
    qb = jnp.where(lane >= HEAD_DIM, q2, 0.0).astype(BF16)
    bias_a = bias_ref[2 * p]
    bias_b = bias_ref[2 * p + 1]
    row_id = i * tq + lax.broadcasted_iota(jnp.int32, (tq, 1), 0)
    ur = lax.broadcasted_iota(jnp.int32, (tk, tk), 0)
    uc = lax.broadcasted_iota(jnp.int32, (tk, tk), 1)
    ustrict = (ur > uc).astype(BF16)

    def body(jj, state):
        ca, cb, acca, accb = state
        j = i - jj
        start = pl.multiple_of(j * tk, tk)
        k2 = k_ref[pl.ds(start, tk), :].astype(BF16)
        v2 = v_ref[pl.ds(start, tk), :].astype(BF16)
        col_id = j * tk + lax.broadcasted_iota(jnp.int32, (1, tk), 1)
        mask = col_id < row_id
        ca, acca = _sb_block(qa, k2, v2, bias_a, mask, ustrict, ca, acca)
        cb, accb = _sb_block(qb, k2, v2, bias_b, mask, ustrict, cb, accb)
        return ca, cb, acca, accb

    zc = jnp.zeros((tq, 1), F32)
    za = jnp.zeros((tq, LANES), F32)
    _, _, acca, accb = lax.fori_loop(0, i + 1, body, (zc, zc, za, za))
    o_ref[...] = jnp.where(lane < HEAD_DIM, acca, accb).astype(o_ref.dtype)


def _sb_prompt(q, k, v, sb_bias, tq=256):
    b, l, w = q.shape
    assert l % tq == 0 and w % LANES == 0
    return pl.pallas_call(
        functools.partial(_sb_prompt_kernel, tq=tq),
        grid=(b, w // LANES, l // tq),
        in_specs=[
            pl.BlockSpec(memory_space=pltpu.SMEM),
            pl.BlockSpec((None, tq, LANES), lambda bi, p, i: (bi, i, p)),
            pl.BlockSpec((None, l, LANES), lambda bi, p, i: (bi, 0, p)),
            pl.BlockSpec((None, l, LANES), lambda bi, p, i: (bi, 0, p)),
        ],
        out_specs=pl.BlockSpec((None, tq, LANES), lambda bi, p, i: (bi, i, p)),
        out_shape=jax.ShapeDtypeStruct((b, l, w), BF16),
        compiler_params=pltpu.CompilerParams(
            dimension_semantics=("arbitrary", "arbitrary", "arbitrary"), vmem_limit_bytes=VMEM_LIMIT),
        name="sb_prompt",
    )(sb_bias.astype(F32), q, k, v)


def _sb_decode_kernel(pt_ref, qbd_ref, biasc_ref, knew_ref, vnew_ref, ka_ref, kb_ref, va_ref, vb_ref,
                      o_ref, carry_scr, acc_scr, *, n_new, n_steps):
    del pt_ref
    j = pl.program_id(1)
    nh = SB_HEADS
    rows = n_new * nh
    tk = knew_ref.shape[0]

    @pl.when(j == 0)
    def _():
        carry_scr[...] = jnp.zeros_like(carry_scr)
        acc_scr[...] = jnp.zeros_like(acc_scr)

    is_new = j == 0
    k_pages = jnp.concatenate([ka_ref[...], kb_ref[...]], axis=0)
    v_pages = jnp.concatenate([va_ref[...], vb_ref[...]], axis=0)
    kt = jnp.where(is_new, knew_ref[...], k_pages)
    vt = jnp.where(is_new, vnew_ref[...], v_pages)
    t_row = lax.broadcasted_iota(jnp.int32, (rows, 1), 0) // nh
    limit = jnp.where(is_new, t_row, tk)
    mask = lax.broadcasted_iota(jnp.int32, (1, tk), 1) < limit
    ur = lax.broadcasted_iota(jnp.int32, (tk, tk), 0)
    uc = lax.broadcasted_iota(jnp.int32, (tk, tk), 1)
    ustrict = (ur > uc).astype(BF16)
    carry, acc = _sb_block(qbd_ref[...], kt, vt, biasc_ref[...], mask, ustrict,
                           carry_scr[...], acc_scr[...])
    carry_scr[...] = carry
    acc_scr[...] = acc

    @pl.when(j == n_steps - 1)
    def _():
        hmask = (lax.broadcasted_iota(jnp.int32, (nh, SB_WIDTH), 1) // HEAD_DIM
                 == lax.broadcasted_iota(jnp.int32, (nh, SB_WIDTH), 0))
        for t in range(n_new):
            blk = jnp.where(hmask, acc[t * nh:(t + 1) * nh, :], 0.0)
            o_ref[t:t + 1, :] = jnp.sum(blk, axis=0, keepdims=True).astype(o_ref.dtype)


def _sb_decode(q, k_new, v_new, cache_k, cache_v, page_table, sb_bias):
    b, t, w = q.shape
    n_pool, page, _ = cache_k.shape
    n_pages = page_table.shape[1]
    nh = SB_HEADS
    assert n_pages % 2 == 0 and t <= 2 * page
    tk = 2 * page
    n_steps = 1 + n_pages // 2
    hmask = (jnp.arange(w)[None, :] // HEAD_DIM == jnp.arange(nh)[:, None]).astype(F32)
    qbd = (q.astype(F32)[:, :, None, :] * hmask[None, None] * (HEAD_DIM ** -0.5)).reshape(b, t * nh, w)
    biasc = jnp.tile(sb_bias.astype(F32), t).reshape(t * nh, 1)
    pad = lambda a: jnp.pad(a.astype(cache_k.dtype), ((0, 0), (0, tk - t), (0, 0)))
    first = lambda bi, j, pt: (pt[bi, n_pages - 2 * jnp.maximum(j, 1)], 0, 0)
    second = lambda bi, j, pt: (pt[bi, n_pages - 2 * jnp.maximum(j, 1) + 1], 0, 0)
    per_b = lambda bi, j, pt: (bi, 0, 0)
    grid_spec = pltpu.PrefetchScalarGridSpec(
        num_scalar_prefetch=1,
        grid=(b, n_steps),
        in_specs=[
            pl.BlockSpec((None, t * nh, w), per_b),
            pl.BlockSpec((t * nh, 1), lambda bi, j, pt: (0, 0)),
            pl.BlockSpec((None, tk, w), per_b),
            pl.BlockSpec((None, tk, w), per_b),
            pl.BlockSpec((None, page, w), first),
            pl.BlockSpec((None, page, w), second),
            pl.BlockSpec((None, page, w), first),
            pl.BlockSpec((None, page, w), second),
        ],
        out_specs=pl.BlockSpec((None, t, w), per_b),
        scratch_shapes=[pltpu.VMEM((t * nh, 1), F32), pltpu.VMEM((t * nh, w), F32)],
    )
    return pl.pallas_call(
        functools.partial(_sb_decode_kernel, n_new=t, n_steps=n_steps),
        grid_spec=grid_spec,
        out_shape=jax.ShapeDtypeStruct((b, t, w), F32),
        compiler_params=pltpu.CompilerParams(dimension_semantics=("arbitrary", "arbitrary"),
                                             vmem_limit_bytes=VMEM_LIMIT),
        name="sb_decode",
    )(page_table.astype(jnp.int32), qbd, biasc, pad(k_new), pad(v_new), cache_k, cache_k, cache_v, cache_v)


DECODE_PAGES_PER_STEP = 8
QROWS = SUBLANES


def _sbd_kernel(pt_ref, q_ref, biasc_ref, knew_ref, vnew_ref, *rest, n_steps, pg):
    del pt_ref
    k_refs, v_refs = rest[:pg], rest[pg:2 * pg]
    o_ref, carry_scr, acc_scr = rest[2 * pg:]
    j = pl.program_id(1)
    nh = SB_HEADS
    page = knew_ref.shape[1]
    rows = nh * QROWS
    q = q_ref[...]
    biasc = biasc_ref[...]
    ur = lax.broadcasted_iota(jnp.int32, (page, page), 0)
    uc = lax.broadcasted_iota(jnp.int32, (page, page), 1)
    ustrict = (ur > uc).astype(BF16)
    col = lax.broadcasted_iota(jnp.int32, (1, page), 1)
    t_row = lax.broadcasted_iota(jnp.int32, (rows, 1), 0) % QROWS

    def scores(k_of, limit):
        z = jnp.concatenate([_dot_nt(q[h], k_of(h)) for h in range(nh)], axis=0) + biasc
        mask = col < limit
        sp = _softplus(z)
        lr = jnp.where(mask, -sp, 0.0)
        lr_hi = lr.astype(BF16)
        lr_lo = (lr - lr_hi.astype(F32)).astype(BF16)
        excl = (jnp.dot(lr_hi, ustrict, preferred_element_type=F32)
                + jnp.dot(lr_lo, ustrict, preferred_element_type=F32))
        return mask, z - sp + excl, excl[:, 0:1] + lr[:, 0:1]

    def weights(mask, logit, carry):
        return jnp.where(mask, jnp.exp(logit + carry), 0.0)

    @pl.when(j == 0)
    def _():
        mask, logit, row_sum = scores(lambda h: knew_ref[h], t_row)
        a = weights(mask, logit, 0.0)
        for h in range(nh):
            acc_scr[h] = _dot(a[h * QROWS:(h + 1) * QROWS], vnew_ref[h])
        carry_scr[...] = row_sum

    order = list(reversed(range(pg)))
    terms = [scores(lambda h, r=k_refs[i]: r[pl.ds(h, page, stride=nh), :], page) for i in order]
    carry = carry_scr[...]
    probs = []
    for mask, logit, row_sum in terms:
        probs.append(weights(mask, logit, carry))
        carry = carry + row_sum
    carry_scr[...] = carry
    for h in range(nh):
        acc = acc_scr[h]
        for a, i in zip(probs, order):
            acc = acc + _dot(a[h * QROWS:(h + 1) * QROWS], v_refs[i][pl.ds(h, page, stride=nh), :])
        acc_scr[h] = acc

    @pl.when(j == n_steps - 1)
    def _():
        o_ref[...] = acc_scr[...]


def _sb_decode_native(q, k_new, v_new, cache_k, cache_v, page_table, sb_bias):
    b, t, w = q.shape
    n_pool, page, nh, hd = cache_k.shape
    n_pages = page_table.shape[1]
    pg = math.gcd(n_pages, DECODE_PAGES_PER_STEP)
    n_steps = n_pages // pg
    assert t <= QROWS and t <= page
    to_heads = lambda a: a.astype(F32).reshape(b, t, nh, hd).transpose(0, 2, 1, 3)
    qh = jnp.pad(to_heads(q) * (hd ** -0.5), ((0, 0), (0, 0), (0, QROWS - t), (0, 0)))
    knew = jnp.pad(to_heads(k_new), ((0, 0), (0, 0), (0, page - t), (0, 0)))
    vnew = jnp.pad(to_heads(v_new), ((0, 0), (0, 0), (0, page - t), (0, 0)))
    biasc = jnp.repeat(sb_bias.astype(F32), QROWS).reshape(nh * QROWS, 1)
    ck = cache_k.reshape(n_pool, page * nh, hd)
    cv = cache_v.reshape(n_pool, page * nh, hd)

    def page_map(i):
        return lambda bi, j, pt: (pt[bi, n_pages - (j + 1) * pg + i], 0, 0)

    per_b = lambda bi, j, pt: (bi, 0, 0, 0)
    page_specs = [pl.BlockSpec((None, page * nh, hd), page_map(i)) for i in range(pg)]
    grid_spec = pltpu.PrefetchScalarGridSpec(
        num_scalar_prefetch=1,
        grid=(b, n_steps),
        in_specs=[pl.BlockSpec((None, nh, QROWS, hd), per_b),
                  pl.BlockSpec((nh * QROWS, 1), lambda bi, j, pt: (0, 0)),
                  pl.BlockSpec((None, nh, page, hd), per_b),
                  pl.BlockSpec((None, nh, page, hd), per_b)] + page_specs + page_specs,
        out_specs=pl.BlockSpec((None, nh, QROWS, hd), per_b),
        scratch_shapes=[pltpu.VMEM((nh * QROWS, 1), F32), pltpu.VMEM((nh, QROWS, hd), F32)],
    )
    out = pl.pallas_call(
        functools.partial(_sbd_kernel, n_steps=n_steps, pg=pg),
        grid_spec=grid_spec,
        out_shape=jax.ShapeDtypeStruct((b, nh, QROWS, hd), F32),
        compiler_params=pltpu.CompilerParams(dimension_semantics=("arbitrary", "arbitrary"),
                                             vmem_limit_bytes=VMEM_LIMIT),
        name="sb_decode",
    )(page_table.astype(jnp.int32), qh, biasc, knew, vnew, *([ck] * pg), *([cv] * pg))
    return out[:, :, :t].transpose(0, 2, 1, 3).reshape(b, t, w)


NEG_BIG = -1e30


def _outproj_router_kernel(x_ref, og_ref, os_ref, wo1_ref, wo2_ref, nw_ref, rw1_ref, rw2_ref, rb_ref,
                           x2_ref, xn_ref, ids_ref, gates_ref, cnt_ref):
    i = pl.program_id(0)
    x2 = (x_ref[...] + jnp.dot(og_ref[...], wo1_ref[...], preferred_element_type=F32)
          + jnp.dot(os_ref[...], wo2_ref[...], preferred_element_type=F32))
    x2_ref[...] = x2
    xn = x2 * lax.rsqrt(jnp.mean(x2 * x2, axis=-1, keepdims=True) + RMS_EPS) * nw_ref[...]
    xn_ref[...] = xn
    x_hi = xn.astype(BF16)
    x_lo = (xn - x_hi.astype(F32)).astype(BF16)
    logits = (jnp.dot(x_hi, rw1_ref[...], preferred_element_type=F32)
              + jnp.dot(x_hi, rw2_ref[...], preferred_element_type=F32)
              + jnp.dot(x_lo, rw1_ref[...], preferred_element_type=F32)) + rb_ref[...]
    tm = logits.shape[0]
    lane = lax.broadcasted_iota(jnp.int32, (tm, LANES), 1)
    lane_f = lane.astype(F32)
    vals, idxs = [], []
    cnt = jnp.zeros((tm, LANES), F32)
    l = logits
    for _ in range(TOP_K):
        m = jnp.max(l, axis=-1, keepdims=True)
        idx = jnp.min(jnp.where(l == m, lane_f, float(LANES)), axis=-1, keepdims=True)
        sel = lane_f == idx
        cnt = cnt + sel.astype(F32)
        l = jnp.where(sel, 2.0 * NEG_BIG, l)
        vals.append(m)
        idxs.append(idx)
    es = [jnp.exp(v - vals[0]) for v in vals]
    den = es[0]
    for e in es[1:]:
        den = den + e
    ids_out = jnp.zeros((tm, LANES), F32)
    gates_out = jnp.zeros((tm, LANES), F32)
    for k in range(TOP_K):
        ids_out = jnp.where(lane == k, idxs[k], ids_out)
        gates_out = jnp.where(lane == k, es[k] / den, gates_out)
    ids_ref[...] = ids_out.astype(jnp.int32)
    gates_ref[...] = gates_out

    @pl.when(i == 0)
    def _():
        cnt_ref[...] = jnp.zeros_like(cnt_ref)

    cnt_ref[...] += jnp.sum(cnt, axis=0, keepdims=True)


def _row_tile(n, candidates):
    for c in candidates:
        if n % c == 0:
            return c
    raise ValueError(f"no row tile among {candidates} divides {n}")


def _outproj_router(x, og, os_, w_out, norm_w, router_w, router_b):
    n, d = x.shape
    tm = _row_tile(n, (384, 256, 128, 64, 32, 16, 8))
    ne = router_w.shape[1]
    wo = w_out.astype(BF16)
    rw = jnp.pad(router_w.astype(F32), ((0, 0), (0, LANES - ne)))
    rw1 = rw.astype(BF16)
    rw2 = (rw - rw1.astype(F32)).astype(BF16)
    rb = jnp.pad(router_b.astype(F32), (0, LANES - ne), constant_values=NEG_BIG).reshape(1, LANES)
    row = lambda i: (i, 0)
    full = lambda i: (0, 0)
    gw = og.shape[1]
    return pl.pallas_call(
        _outproj_router_kernel,
        grid=(n // tm,),
        in_specs=[pl.BlockSpec((tm, d), row), pl.BlockSpec((tm, gw), row), pl.BlockSpec((tm, os_.shape[1]), row),
                  pl.BlockSpec((gw, d), full), pl.BlockSpec((os_.shape[1], d), full), pl.BlockSpec((1, d), full),
                  pl.BlockSpec((d, LANES), full), pl.BlockSpec((d, LANES), full), pl.BlockSpec((1, LANES), full)],
        out_specs=[pl.BlockSpec((tm, d), row), pl.BlockSpec((tm, d), row), pl.BlockSpec((tm, LANES), row),
                   pl.BlockSpec((tm, LANES), row), pl.BlockSpec((1, LANES), full)],
        out_shape=[jax.ShapeDtypeStruct((n, d), F32), jax.ShapeDtypeStruct((n, d), F32),
                   jax.ShapeDtypeStruct((n, LANES), jnp.int32), jax.ShapeDtypeStruct((n, LANES), F32),
                   jax.ShapeDtypeStruct((1, LANES), F32)],
        compiler_params=pltpu.CompilerParams(dimension_semantics=("arbitrary",), vmem_limit_bytes=VMEM_LIMIT),
        name="outproj_router",
    )(x, og, os_, wo[:gw], wo[gw:], norm_w.reshape(1, d).astype(F32), rw1, rw2, rb)


def _positions_kernel(ids_ref, base_ref, pos_ref, run_scr):
    i = pl.program_id(0)

    @pl.when(i == 0)
    def _():
        run_scr[...] = jnp.zeros_like(run_scr)

    tm = ids_ref.shape[0]
    ids = ids_ref[...].astype(F32)
    lane = lax.broadcasted_iota(jnp.int32, (tm, LANES), 1)
    lane_f = lane.astype(F32)
    r = lax.broadcasted_iota(jnp.int32, (tm, tm), 0)
    c = lax.broadcasted_iota(jnp.int32, (tm, tm), 1)
    below = (r > c).astype(BF16)
    offset = run_scr[...] + base_ref[...]
    pos_out = jnp.zeros((tm, LANES), F32)
    for k in range(TOP_K):
        oh = lane_f == ids[:, k:k + 1]
        ohf = oh.astype(F32)
        rank = jnp.dot(below, ohf.astype(BF16), preferred_element_type=F32) + offset
        pk = jnp.sum(jnp.where(oh, rank, 0.0), axis=-1, keepdims=True)
        pos_out = jnp.where(lane == k, pk, pos_out)
        offset = offset + jnp.sum(ohf, axis=0, keepdims=True)
    run_scr[...] = offset - base_ref[...]
    pos_ref[...] = pos_out.astype(jnp.int32)


def _positions(ids, base):
    n = ids.shape[0]
    tm = _row_tile(n, (384, 256, 128, 64, 32, 16, 8))
    return pl.pallas_call(
        _positions_kernel,
        grid=(n // tm,),
        in_specs=[pl.BlockSpec((tm, LANES), lambda i: (i, 0)), pl.BlockSpec((1, LANES), lambda i: (0, 0))],
        out_specs=pl.BlockSpec((tm, LANES), lambda i: (i, 0)),
        out_shape=jax.ShapeDtypeStruct((n, LANES), jnp.int32),
        scratch_shapes=[pltpu.VMEM((1, LANES), F32)],
        compiler_params=pltpu.CompilerParams(dimension_semantics=("arbitrary",)),
        name="positions",
    )(ids, base)


def _dispatch_kernel(pos_ref, xn_ref, zero_ref, xs_ref, sem):
    del zero_ref
    tm = xn_ref.shape[0]

    def row_copy(r, k):
        dst = pos_ref[0, r * TOP_K + k]
        return pltpu.make_async_copy(xn_ref.at[pl.ds(r, 1), :], xs_ref.at[pl.ds(dst, 1), :], sem)

    def body(r, carry):
        for k in range(TOP_K):
            row_copy(r, k).start()
        return carry

    lax.fori_loop(0, tm, body, 0)
    for _ in range(TOP_K):
        pltpu.make_async_copy(xn_ref, xs_ref.at[pl.ds(0, tm), :], sem).wait()


def _dispatch(xn, pos, n_slots):
    n, d = xn.shape
    tm = _row_tile(n, (384, 256, 128, 64, 32, 16, 8))
    pos2 = pos[:, :TOP_K].reshape(n // tm, 1, tm * TOP_K)
    zeros = jnp.zeros((n_slots, d), xn.dtype)
    return pl.pallas_call(
        _dispatch_kernel,
        grid=(n // tm,),
        in_specs=[pl.BlockSpec((None, 1, tm * TOP_K), lambda i: (i, 0, 0), memory_space=pltpu.SMEM),
                  pl.BlockSpec((tm, d), lambda i: (i, 0)),
                  pl.BlockSpec(memory_space=pl.ANY)],
        out_specs=pl.BlockSpec(memory_space=pl.ANY),
        out_shape=jax.ShapeDtypeStruct((n_slots, d), xn.dtype),
        scratch_shapes=[pltpu.SemaphoreType.DMA(())],
        input_output_aliases={2: 0},
        compiler_params=pltpu.CompilerParams(dimension_semantics=("arbitrary",)),
        name="dispatch",
    )(pos2, xn, zeros)


def _moe_kernel(be_ref, nu_ref, x_ref, wup_ref, wdn_ref, bup_ref, bdn_ref, y_ref, wup_bf, wdn2_f, wdn2_bf):
    b = pl.program_id(0)
    new_expert = jnp.logical_or(b == 0, be_ref[b] != be_ref[jnp.maximum(b - 1, 0)])

    @pl.when(jnp.logical_and(new_expert, b < nu_ref[0]))
    def _():
        wup_bf[...] = wup_ref[...].astype(BF16)
        ff = wdn_ref.shape[0]
        for c in range(wdn2_f.shape[0]):
            cols = slice(c * LANES, (c + 1) * LANES)
            wdn2_f[c, pl.ds(0, ff, stride=2), :] = wdn_ref[:, cols]
            wdn2_f[c, pl.ds(1, ff, stride=2), :] = wdn_ref[:, cols]
            wdn2_bf[:, cols] = wdn2_f[c].astype(BF16)

    @pl.when(b < nu_ref[0])
    def _():
        x = x_ref[...].astype(BF16)
        hu = jnp.dot(x, wup_bf[...], preferred_element_type=F32) + bup_ref[...]
        nxt = pltpu.roll(hu, hu.shape[1] - 1, axis=1)
        g = jnp.minimum(hu, SWIGLU_LIMIT)
        u = jnp.clip(nxt, -SWIGLU_LIMIT, SWIGLU_LIMIT)
        act = (u + 1.0) * g * _sigmoid(SWIGLU_ALPHA * g)
        even = lax.broadcasted_iota(jnp.int32, hu.shape, 1) % 2 == 0
        act = jnp.where(even, act, 0.0)
        y_ref[...] = jnp.dot(act.astype(BF16), wdn2_bf[...], preferred_element_type=F32) + bdn_ref[...]

    @pl.when(b >= nu_ref[0])
    def _():
        y_ref[...] = jnp.zeros_like(y_ref)


def _moe_mm(xs, block_e, n_used, w_up, w_down, b_up, b_down, bm):
    n_slots, d = xs.shape
    ff2 = w_up.shape[2]
    n_blocks = n_slots // bm
    wsel = lambda b, be, nu: (be[b], 0, 0)
    grid_spec = pltpu.PrefetchScalarGridSpec(
        num_scalar_prefetch=2,
        grid=(n_blocks,),
        in_specs=[pl.BlockSpec((bm, d), lambda b, be, nu: (b, 0)),
                  pl.BlockSpec((None, d, ff2), wsel), pl.BlockSpec((None, ff2 // 2, d), wsel),
                  pl.BlockSpec((None, 1, ff2), wsel), pl.BlockSpec((None, 1, d), wsel)],
        out_specs=pl.BlockSpec((bm, d), lambda b, be, nu: (b, 0)),
        scratch_shapes=[pltpu.VMEM((d, ff2), BF16), pltpu.VMEM((d // LANES, ff2, LANES), F32),
                        pltpu.VMEM((ff2, d), BF16)],
    )
    return pl.pallas_call(
        _moe_kernel,
        grid_spec=grid_spec,
        out_shape=jax.ShapeDtypeStruct((n_slots, d), F32),
        compiler_params=pltpu.CompilerParams(dimension_semantics=("arbitrary",), vmem_limit_bytes=VMEM_LIMIT),
        name="moe_mm",
    )(block_e, n_used, xs, w_up, w_down, b_up[:, None, :], b_down[:, None, :])


def _combine_kernel(pos_ref, x2_ref, gates_ref, nw_ref, ys_ref, y_ref, buf, sem):
    tm = x2_ref.shape[0]

    def row_copy(r, k):
        src = pos_ref[0, r * TOP_K + k]
        return pltpu.make_async_copy(ys_ref.at[pl.ds(src, 1), :], buf.at[k, pl.ds(r, 1), :], sem)

    def body(r, carry):
        for k in range(TOP_K):
            row_copy(r, k).start()
        return carry

    lax.fori_loop(0, tm, body, 0)
    for k in range(TOP_K):
        pltpu.make_async_copy(ys_ref.at[pl.ds(0, tm), :], buf.at[k], sem).wait()
    gates = gates_ref[...]
    y = x2_ref[...]
    for k in range(TOP_K):
        y = y + buf[k] * gates[:, k:k + 1]
    y_ref[...] = y * lax.rsqrt(jnp.mean(y * y, axis=-1, keepdims=True) + RMS_EPS) * nw_ref[...]


def _combine(ys, pos, gates, x2, norm_w):
    n, d = x2.shape
    tm = _row_tile(n, (384, 256, 128, 64, 32, 16, 8))
    pos2 = pos[:, :TOP_K].reshape(n // tm, 1, tm * TOP_K)
    row = lambda i: (i, 0)
    return pl.pallas_call(
        _combine_kernel,
        grid=(n // tm,),
        in_specs=[pl.BlockSpec((None, 1, tm * TOP_K), lambda i: (i, 0, 0), memory_space=pltpu.SMEM),
                  pl.BlockSpec((tm, d), row), pl.BlockSpec((tm, LANES), row),
                  pl.BlockSpec((1, d), lambda i: (0, 0)), pl.BlockSpec(memory_space=pl.ANY)],
        out_specs=pl.BlockSpec((tm, d), row),
        out_shape=jax.ShapeDtypeStruct((n, d), F32),
        scratch_shapes=[pltpu.VMEM((TOP_K, tm, d), F32), pltpu.SemaphoreType.DMA(())],
        compiler_params=pltpu.CompilerParams(dimension_semantics=("arbitrary",), vmem_limit_bytes=VMEM_LIMIT),
        name="combine",
    )(pos2, x2, gates, norm_w.reshape(1, d).astype(F32), ys)


def _moe(x2, xn, ids, gates, counts, w_up, b_up, w_down, b_down, norm_final, bm=256):
    n, d = x2.shape
    ne = w_up.shape[0]
    n_pairs = n * TOP_K
    n_blocks = -(-n_pairs // bm) + ne
    n_slots = n_blocks * bm
    cnt = counts[0, :ne].astype(jnp.int32)
    padded = (cnt + bm - 1) // bm * bm
    ends = jnp.cumsum(padded)
    base = jnp.pad((ends - padded).astype(F32), (0, LANES - ne)).reshape(1, LANES)
    starts = jnp.arange(n_blocks, dtype=jnp.int32) * bm
    block_e = jnp.minimum(jnp.sum((ends[None, :] <= starts[:, None]).astype(jnp.int32), axis=1), ne - 1)
    n_used = (ends[-1:] // bm).astype(jnp.int32)
    pos = _positions(ids, base)
    xs = _dispatch(xn, pos, n_slots)
    ys = _moe_mm(xs, block_e, n_used, w_up.astype(F32), w_down.astype(F32),
                 b_up.astype(F32), b_down.astype(F32), bm)
    return _combine(ys, pos, gates, x2, norm_final)


def kernel(x_prompt, x_sample, cache_k, cache_v, page_table, state_conv, state_ssm, norm_mix, w_in, conv_w, a_log, dt_bias, gdn_norm, sb_bias, w_out, norm_ffn, router_w, router_b, w_up, b_up, w_down, b_down, norm_final):
    assert w_in.shape[0] == 1, "single-layer trunk"
    bp, lp, d = x_prompt.shape
    bs, ls, _ = x_sample.shape
    assert lp % GDN_CHUNK == 0 and CONV_WIDTH - 1 <= ls <= GDN_CHUNK
    n_p = bp * lp
    x_all = jnp.concatenate([x_prompt.reshape(n_p, d), x_sample.reshape(bs * ls, d)], axis=0)
    u, z, ba, q_s, k_s, v_s = _inproj(x_all, norm_mix[0], w_in[0])

    up, zp, bap, qp, kp, vp = (t[:n_p].reshape(bp, lp, -1) for t in (u, z, ba, q_s, k_s, v_s))
    og_p, ssm_p = _gdnb(up, zp, bap, jnp.zeros((bp, CONV_WIDTH - 1, CONV_DIM), F32),
                       jnp.zeros((bp, GDN_HEADS, HEAD_DIM, HEAD_DIM), F32),
                       conv_w[0], a_log[0], dt_bias[0], gdn_norm[0], valid=GDN_CHUNK)
    os_p = _sb_prompt(qp, kp, vp, sb_bias[0])

    us, zs, bas, qs, ks, vs = (t[n_p:].reshape(bs, ls, -1) for t in (u, z, ba, q_s, k_s, v_s))
    padl = lambda t: jnp.pad(t, ((0, 0), (0, GDN_CHUNK - ls), (0, 0)))
    og_s, ssm_s = _gdnb(padl(us), padl(zs), padl(bas), state_conv[0], state_ssm[0],
                       conv_w[0], a_log[0], dt_bias[0], gdn_norm[0], valid=ls)
    os_s = _sb_decode_native(qs, ks, vs, cache_k[0], cache_v[0], page_table, sb_bias[0])

    og = jnp.concatenate([og_p.reshape(n_p, GDN_WIDTH), og_s[:, :ls].reshape(bs * ls, GDN_WIDTH)], axis=0)
    os_ = jnp.concatenate([os_p.reshape(n_p, SB_WIDTH), os_s.reshape(bs * ls, SB_WIDTH).astype(BF16)], axis=0)
    x2, xn, ids, gates, counts = _outproj_router(x_all, og, os_, w_out[0], norm_ffn[0], router_w[0], router_b[0])
    y = _moe(x2, xn, ids, gates, counts, w_up[0], b_up[0], w_down[0], b_down[0], norm_final)

    heads = lambda t, b, l: t.reshape(1, b, l, SB_HEADS, HEAD_DIM)
    tail = CONV_WIDTH - 1
    return (y[:n_p].reshape(bp, lp, d), y[n_p:].reshape(bs, ls, d),
            heads(kp, bp, lp), heads(vp, bp, lp), up[None, :, lp - tail:, :], ssm_p[None],
            heads(ks, bs, ls), heads(vs, bs, ls), us[None, :, ls - tail:, :], ssm_s[None])
```
